```python
import jax, jax.numpy as jnp
from jax import lax
import numpy as np

D_MODEL = 4096
BATCH = 4
SEQ = 4096
DEPTH = 2
DEC_BATCH = 8
DEC_SEQ = 64
PAST_LEN = 1024

CHUNK = 64
Q_BLOCK = 128
MLA_HEADS = D_MODEL // 256
Q_LORA = D_MODEL // 4
KV_LORA = 512
NOPE_DIM = 128
ROPE_DIM = 64
V_DIM = 128
ROPE_THETA = 10000.0
SB_HEADS = D_MODEL // 256
SB_DIM = 128
MLA_WIDTH = MLA_HEADS * V_DIM
SB_WIDTH = SB_HEADS * SB_DIM
MIX_WIDTH = MLA_WIDTH + SB_WIDTH
D_FF = 4 * D_MODEL
IN_SPLITS = (Q_LORA,
             Q_LORA + KV_LORA,
             Q_LORA + KV_LORA + ROPE_DIM,
             Q_LORA + KV_LORA + ROPE_DIM + SB_WIDTH,
             Q_LORA + KV_LORA + ROPE_DIM + 2 * SB_WIDTH)
IN_WIDTH = Q_LORA + KV_LORA + ROPE_DIM + 3 * SB_WIDTH
EPS = 1e-6
NEG_INF = -1e30

kernel_name = "hymba_mla_stickbreaking_stream_step"


def rms_norm(x, g):
    xf = x.astype(jnp.float32)
    y = xf * lax.rsqrt(jnp.mean(xf * xf, axis=-1, keepdims=True) + EPS)
    return (y * g.astype(jnp.float32)).astype(x.dtype)


def apply_rope(x, pos):
    half = ROPE_DIM // 2
    inv_freq = jnp.power(ROPE_THETA, -jnp.arange(half, dtype=jnp.float32) / half)
    ang = pos.astype(jnp.float32)[:, None] * inv_freq[None, :]
    cos = jnp.cos(ang)[None, :, None, :]
    sin = jnp.sin(ang)[None, :, None, :]
    xf = x.astype(jnp.float32)
    x1, x2 = xf[..., :half], xf[..., half:]
    return jnp.concatenate([x1 * cos - x2 * sin, x1 * sin + x2 * cos], axis=-1).astype(x.dtype)


def sweep_query_blocks(fn, q, q_pos):
    B, T = q.shape[0], q.shape[1]
    if T <= Q_BLOCK:
        return fn(q, q_pos)
    nb = T // Q_BLOCK
    qb = q.reshape(B, nb, Q_BLOCK, *q.shape[2:]).swapaxes(0, 1)
    pb = q_pos.reshape(nb, Q_BLOCK)
    out = lax.map(lambda a: fn(a[0], a[1]), (qb, pb))
    return out.swapaxes(0, 1).reshape(B, T, *out.shape[3:])


def chunk_causal_softmax(q, q_pos, k, v, k_pos):
    scale = (NOPE_DIM + ROPE_DIM) ** -0.5
    s = jnp.einsum("bqhd,bkhd->bhqk", q, k).astype(jnp.float32) * scale
    visible = (k_pos[None, :] // CHUNK) <= (q_pos[:, None] // CHUNK)
    s = jnp.where(visible[None, None], s, NEG_INF)
    p = jax.nn.softmax(s, axis=-1)
    return jnp.einsum("bhqk,bkhd->bqhd", p.astype(v.dtype), v)


def stick_breaking(q, q_pos, k, v, k_pos):
    z = jnp.einsum("bqhd,bkhd->bhqk", q, k).astype(jnp.float32) * (SB_DIM ** -0.5)
    before = k_pos[None, :] < q_pos[:, None]
    log_beta = jax.nn.log_sigmoid(z)
    log_fail = jnp.where(before[None, None], jax.nn.log_sigmoid(-z), 0.0)
    later_fail = lax.cumsum(log_fail, axis=3, reverse=True) - log_fail
    w = jnp.where(before[None, None], jnp.exp(log_beta + later_fail), 0.0)
    return jnp.einsum("bhqk,bkhd->bqhd", w.astype(v.dtype), v)


def token_mixers(xn, pos, past, p):
    B, T, _ = xn.shape
    proj = xn @ p["w_in"]
    q_lat, kv_lat, kr_raw, sq, sk, sv = jnp.split(proj, IN_SPLITS, axis=-1)
    q = (rms_norm(q_lat, p["g_q_a"]) @ p["w_q_b"]).reshape(B, T, MLA_HEADS, NOPE_DIM + ROPE_DIM)
    q_mla = jnp.concatenate([rms_norm(q[..., :NOPE_DIM], p["g_q_nope"]),
                             apply_rope(rms_norm(q[..., NOPE_DIM:], p["g_q_rope"]), pos)], axis=-1)
    c_kv = rms_norm(kv_lat, p["g_kv_a"])
    k_rope = apply_rope(rms_norm(kr_raw, p["g_k_rope"])[:, :, None, :], pos)[:, :, 0, :]
    sb_q = sq.reshape(B, T, SB_HEADS, SB_DIM)
    sb_k = sk.reshape(B, T, SB_HEADS, SB_DIM)
    sb_v = sv.reshape(B, T, SB_HEADS, SB_DIM)
    if past is None:
        c_all, kr_all, sbk_all, sbv_all, k_pos = c_kv, k_rope, sb_k, sb_v, pos
    else:
        p_ckv, p_kr, p_sbk, p_sbv = past
        c_all = jnp.concatenate([p_ckv, c_kv], axis=1)
        kr_all = jnp.concatenate([p_kr, k_rope], axis=1)
        sbk_all = jnp.concatenate([p_sbk, sb_k], axis=1)
        sbv_all = jnp.concatenate([p_sbv, sb_v], axis=1)
        k_pos = jnp.concatenate([jnp.arange(p_ckv.shape[1], dtype=jnp.int32), pos])
    S = c_all.shape[1]
    kv = (c_all @ p["w_kv_b"]).reshape(B, S, MLA_HEADS, NOPE_DIM + V_DIM)
    k_mla = jnp.concatenate([rms_norm(kv[..., :NOPE_DIM], p["g_k_nope"]),
                             jnp.broadcast_to(kr_all[:, :, None, :], (B, S, MLA_HEADS, ROPE_DIM))], axis=-1)
    v_mla = kv[..., NOPE_DIM:]
    mla_out = sweep_query_blocks(lambda qb, pb: chunk_causal_softmax(qb, pb, k_mla, v_mla, k_pos), q_mla, pos)
    sb_out = sweep_query_blocks(lambda qb, pb: stick_breaking(qb, pb, sbk_all, sbv_all, k_pos), sb_q, pos)
    merged = jnp.concatenate([rms_norm(mla_out.reshape(B, T, MLA_WIDTH), p["g_out_mla"]),
                              rms_norm(sb_out.reshape(B, T, SB_WIDTH), p["g_out_sb"])], axis=-1)
    return merged @ p["w_o"], (c_kv, k_rope, sb_k, sb_v)


def trunk_layer(x, pos, past, p):
    mix, new_rows = token_mixers(rms_norm(x, p["g_attn"]), pos, past, p)
    h = x + mix
    u = jnp.maximum(rms_norm(h, p["g_mlp"]) @ p["w_up"], 0.0)
    return h + (u * u) @ p["w_down"], new_rows


def setup_inputs(seed: int = 0) -> dict:
    key = jax.random.key(seed)
    ks = jax.random.split(key, 26)

    def nrm(k, shape, scale=1.0):
        return jax.random.normal(k, shape, jnp.float32) * scale

    def gain(k, shape):
        return 1.0 + 0.02 * jax.random.normal(k, shape, jnp.float32)

    return {
        "x_prompt": nrm(ks[0], (BATCH, SEQ, D_MODEL)),
        "x_sample": nrm(ks[1], (DEC_BATCH, DEC_SEQ, D_MODEL)),
        "cache_mla_ckv": nrm(ks[2], (DEPTH, DEC_BATCH, PAST_LEN, KV_LORA)),
        "cache_mla_krope": nrm(ks[3], (DEPTH, DEC_BATCH, PAST_LEN, ROPE_DIM)),
        "cache_sb_k": nrm(ks[4], (DEPTH, DEC_BATCH, PAST_LEN, SB_HEADS, SB_DIM)),
        "cache_sb_v": nrm(ks[5], (DEPTH, DEC_BATCH, PAST_LEN, SB_HEADS, SB_DIM)),
        "g_attn": gain(ks[6], (DEPTH, D_MODEL)),
        "w_in": nrm(ks[7], (DEPTH, D_MODEL, IN_WIDTH), D_MODEL ** -0.5),
        "g_q_a": gain(ks[8], (DEPTH, Q_LORA)),
        "w_q_b": nrm(ks[9], (DEPTH, Q_LORA, MLA_HEADS * (NOPE_DIM + ROPE_DIM)), Q_LORA ** -0.5),
        "g_kv_a": gain(ks[10], (DEPTH, KV_LORA)),
        "w_kv_b": nrm(ks[11], (DEPTH, KV_LORA, MLA_HEADS * (NOPE_DIM + V_DIM)), KV_LORA ** -0.5),
        "g_q_nope": gain(ks[12], (DEPTH, NOPE_DIM)),
        "g_q_rope": gain(ks[13], (DEPTH, ROPE_DIM)),
        "g_k_nope": gain(ks[14], (DEPTH, NOPE_DIM)),
        "g_k_rope": gain(ks[15], (DEPTH, ROPE_DIM)),
        "g_out_mla": gain(ks[16], (DEPTH, MLA_WIDTH)),
        "g_out_sb": gain(ks[17], (DEPTH, SB_WIDTH)),
        "w_o": nrm(ks[18], (DEPTH, MIX_WIDTH, D_MODEL), MIX_WIDTH ** -0.5),
        "g_mlp": gain(ks[19], (DEPTH, D_MODEL)),
        "w_up": nrm(ks[20], (DEPTH, D_MODEL, D_FF), D_MODEL ** -0.5),
        "w_down": nrm(ks[21], (DEPTH, D_FF, D_MODEL), D_FF ** -0.5),
    }


def reference(x_prompt, x_sample, cache_mla_ckv, cache_mla_krope, cache_sb_k, cache_sb_v,
              g_attn, w_in, g_q_a, w_q_b, g_kv_a, w_kv_b, g_q_nope, g_q_rope, g_k_nope, g_k_rope,
              g_out_mla, g_out_sb, w_o, g_mlp, w_up, w_down):
    past_len = cache_mla_ckv.shape[2]
    pos_prompt = jnp.arange(x_prompt.shape[1], dtype=jnp.int32)
    pos_sample = past_len + jnp.arange(x_sample.shape[1], dtype=jnp.int32)
    hp, hs = x_prompt, x_sample
    rows_p = ([], [], [], [])
    rows_s = ([], [], [], [])
    for l in range(DEPTH):
        p = {"g_attn": g_attn[l], "w_in": w_in[l], "g_q_a": g_q_a[l], "w_q_b": w_q_b[l],
             "g_kv_a": g_kv_a[l], "w_kv_b": w_kv_b[l], "g_q_nope": g_q_nope[l], "g_q_rope": g_q_rope[l],
             "g_k_nope": g_k_nope[l], "g_k_rope": g_k_rope[l], "g_out_mla": g_out_mla[l],
             "g_out_sb": g_out_sb[l], "w_o": w_o[l], "g_mlp": g_mlp[l], "w_up": w_up[l],
             "w_down": w_down[l]}
        hp, new_p = trunk_layer(hp, pos_prompt, None, p)
        past = (cache_mla_ckv[l], cache_mla_krope[l], cache_sb_k[l], cache_sb_v[l])
        hs, new_s = trunk_layer(hs, pos_sample, past, p)
        for i in range(4):
            rows_p[i].append(new_p[i])
            rows_s[i].append(new_s[i])
    return (hp, hs,
            jnp.stack(rows_p[0]), jnp.stack(rows_p[1]), jnp.stack(rows_p[2]), jnp.stack(rows_p[3]),
            jnp.stack(rows_s[0]), jnp.stack(rows_s[1]), jnp.stack(rows_s[2]), jnp.stack(rows_s[3]))
```

```python
import functools

import jax
import jax.numpy as jnp
from jax import lax
from jax.experimental import pallas as pl
from jax.experimental.pallas import tpu as pltpu

EPS = 1e-6
NEG_INF = -1e30
CHUNK = 64
ROPE_THETA = 10000.0
LANES = 128
MXU_DIM = 256
VMEM_LIMIT = 56 * 1024 * 1024

F32 = jnp.float32
BF16 = jnp.bfloat16


def _pick(n, cap, mult=LANES):
    if n <= cap:
        return n
    best = None
    for d in range(mult, cap + 1, mult):
        if n % d == 0:
            best = d
    assert best is not None, (n, cap, mult)
    return best


def _params(*sem):
    return pltpu.CompilerParams(dimension_semantics=sem, vmem_limit_bytes=VMEM_LIMIT)


def _rms(x, g):
    return x * lax.rsqrt(jnp.mean(x * x, axis=-1, keepdims=True) + EPS) * g


def _norm_cast_kernel(*refs, n_in):
    o_ref = refs[2 * n_in]
    off = 0
    for x_ref, g_ref in zip(refs[:n_in], refs[n_in:2 * n_in]):
        w = x_ref.shape[1]
        o_ref[:, off:off + w] = _rms(x_ref[...], g_ref[...]).astype(o_ref.dtype)
        off += w


def norm_cast(xs, gs, tm=256):
    m = xs[0].shape[0]
    tm = _pick(m, tm, 8)
    widths = [x.shape[1] for x in xs]
    n = len(xs)
    return pl.pallas_call(
        functools.partial(_norm_cast_kernel, n_in=n),
        grid=(m // tm,),
        in_specs=[pl.BlockSpec((tm, w), lambda i: (i, 0)) for w in widths]
        + [pl.BlockSpec((1, w), lambda i: (0, 0)) for w in widths],
        out_specs=pl.BlockSpec((tm, sum(widths)), lambda i: (i, 0)),
        out_shape=jax.ShapeDtypeStruct((m, sum(widths)), BF16),
        compiler_params=_params("parallel"),
        name="norm_cast",
    )(*xs, *[g.reshape(1, -1) for g in gs])


def _mm_kernel(*refs, nk, act, has_res, layouts):
    x_ref, w_ref = refs[0], refs[1]
    p = 2
    res_ref = None
    if has_res:
        res_ref = refs[p]
        p += 1
    out_refs = refs[p:p + len(layouts)]
    acc_ref = refs[p + len(layouts)] if nk > 1 else None

    def finish(r):
        if act == "relu2":
            r = jnp.maximum(r, 0.0)
            r = r * r
        if has_res:
            r = r + res_ref[...]
        for o_ref, layout in zip(out_refs, layouts):
            if layout == "tok":
                o_ref[...] = r.astype(o_ref.dtype)
            else:
                for hh in range(o_ref.shape[0]):
                    o_ref[hh] = r[:, hh * LANES:(hh + 1) * LANES].astype(o_ref.dtype)

    part = jnp.dot(x_ref[...], w_ref[...], preferred_element_type=F32)
    if nk == 1:
        finish(part)
    else:
        k = pl.program_id(2)

        @pl.when(k == 0)
        def _():
            acc_ref[...] = part

        @pl.when(jnp.logical_and(k > 0, k < nk - 1))
        def _():
            acc_ref[...] += part

        @pl.when(k == nk - 1)
        def _():
            finish(acc_ref[...] + part)


def matmul(x, w, outs, act=None, res=None, tm=1024, tn=1024, tk=None):
    m, kdim = x.shape
    n = w.shape[1]
    tm = _pick(m, tm, 8)
    tn = _pick(n, tn)
    tk = kdim if tk is None else _pick(kdim, tk)
    nk = kdim // tk
    grid = (m // tm, n // tn, nk)
    in_specs = [pl.BlockSpec((tm, tk), lambda i, j, k: (i, k)),
                pl.BlockSpec((tk, tn), lambda i, j, k: (k, j))]
    args = [x, w]
    if res is not None:
        in_specs.append(pl.BlockSpec((tm, tn), lambda i, j, k: (i, j)))
        args.append(res)
    out_specs, out_shapes = [], []
    for dtype, layout in outs:
        if layout == "tok":
            out_specs.append(pl.BlockSpec((tm, tn), lambda i, j, k: (i, j)))
            out_shapes.append(jax.ShapeDtypeStruct((m, n), dtype))
        else:
            out_specs.append(pl.BlockSpec((tn // LANES, tm, LANES), lambda i, j, k: (j, i, 0)))
            out_shapes.append(jax.ShapeDtypeStruct((n // LANES, m, LANES), dtype))
    scratch = [pltpu.VMEM((tm, tn), F32)] if nk > 1 else []
    return pl.pallas_call(
        functools.partial(_mm_kernel, nk=nk, act=act, has_res=res is not None,
                          layouts=tuple(l for _, l in outs)),
        grid=grid,
        in_specs=in_specs,
        out_specs=out_specs,
        out_shape=out_shapes,
        scratch_shapes=scratch,
        compiler_params=_params("parallel", "parallel", "arbitrary"),
        name="matmul",
    )(*args)


def _latent_kernel(lat_ref, gqa_ref, wq_ref, qrow_ref, gkva_ref, krow_ref, tab_ref,
                   q_ref, ckv_ref, ckvb_ref, kr_ref, krb_ref, *, ql, kvl, heads, rope):
    lat = lat_ref[...]
    tab = tab_ref[...]

    def rotate(x, row):
        t = _rms(x, row) * tab
        return t + pltpu.roll(t, rope, axis=1)

    qn = _rms(lat[:, :ql], gqa_ref[...]).astype(BF16)
    q = jnp.dot(qn, wq_ref[...], preferred_element_type=F32)
    qrow = qrow_ref[...]
    for h in range(heads):
        base = h * 2 * LANES
        nope = _rms(q[:, base:base + LANES], qrow[:, :LANES])
        q_ref[h, :, :LANES] = nope.astype(BF16)
        q_ref[h, :, LANES:] = rotate(q[:, base + LANES:base + 2 * LANES], qrow[:, LANES:]).astype(BF16)

    ckv = _rms(lat[:, ql:ql + kvl], gkva_ref[...])
    ckv_ref[...] = ckv
    ckvb_ref[...] = ckv.astype(BF16)

    kr = rotate(lat[:, ql + kvl:ql + kvl + LANES], krow_ref[...])
    kr_ref[...] = kr[:, :rope]
    lane = lax.broadcasted_iota(jnp.int32, kr.shape, 1)
    krb_ref[...] = jnp.where(lane < rope, kr, 0.0).astype(BF16)


def latent_post(lat, gqa, wq, qrow, gkva, krow, tab, *, ql, kvl, heads, rope, tm):
    m = lat.shape[0]
    t = tab.shape[0]
    tm = min(tm, t)
    assert t % tm == 0 and m % tm == 0
    nt = t // tm
    full = lambda a: pl.BlockSpec(a.shape, lambda i: (0,) * a.ndim)
    return pl.pallas_call(
        functools.partial(_latent_kernel, ql=ql, kvl=kvl, heads=heads, rope=rope),
        grid=(m // tm,),
        in_specs=[pl.BlockSpec((tm, lat.shape[1]), lambda i: (i, 0)),
                  full(gqa), full(wq), full(qrow), full(gkva), full(krow),
                  pl.BlockSpec((tm, LANES), lambda i: (i % nt, 0))],
        out_specs=[pl.BlockSpec((heads, tm, 2 * LANES), lambda i: (0, i, 0)),
                   pl.BlockSpec((tm, kvl), lambda i: (i, 0)),
                   pl.BlockSpec((tm, kvl), lambda i: (i, 0)),
                   pl.BlockSpec((tm, rope), lambda i: (i, 0)),
                   pl.BlockSpec((tm, LANES), lambda i: (i, 0))],
        out_shape=[jax.ShapeDtypeStruct((heads, m, 2 * LANES), BF16),
                   jax.ShapeDtypeStruct((m, kvl), F32),
                   jax.ShapeDtypeStruct((m, kvl), BF16),
                   jax.ShapeDtypeStruct((m, rope), F32),
                   jax.ShapeDtypeStruct((m, LANES), BF16)],
        compiler_params=_params("parallel"),
        name="latent_post",
    )(lat, gqa, wq, qrow, gkva, krow, tab)


def _kv_expand_kernel(c_ref, wk_ref, wv_ref, g_ref, k_ref, v_ref, *, heads):
    c = c_ref[...]
    k = jnp.dot(c, wk_ref[...], preferred_element_type=F32)
    v = jnp.dot(c, wv_ref[...], preferred_element_type=F32)
    g = g_ref[...]
    for h in range(heads):
        sl = slice(h * LANES, (h + 1) * LANES)
        k_ref[h] = _rms(k[:, sl], g).astype(BF16)
        v_ref[h] = v[:, sl].astype(BF16)


def kv_expand(c, wk, wv, g, *, heads, tm=512):
    rows = c.shape[0]
    tm = _pick(rows, tm, 16)
    full = lambda a: pl.BlockSpec(a.shape, lambda i: (0,) * a.ndim)
    hm = jax.ShapeDtypeStruct((heads, rows, LANES), BF16)
    return pl.pallas_call(
        functools.partial(_kv_expand_kernel, heads=heads),
        grid=(rows // tm,),
        in_specs=[pl.BlockSpec((tm, c.shape[1]), lambda i: (i, 0)), full(wk), full(wv), full(g)],
        out_specs=[pl.BlockSpec((heads, tm, LANES), lambda i: (0, i, 0))] * 2,
        out_shape=[hm, hm],
        compiler_params=_params("parallel"),
        name="kv_expand",
    )(c, wk, wv, g)


def _lane_tile(x, reps):
    return x if reps == 1 else jnp.concatenate([x] * reps, axis=1)


def _mla_kernel(q_ref, k_ref, kr_ref, v_ref, o_ref, m_ref, l_ref, acc_ref, *, tq, tk, q_off, s_len):
    i = pl.program_id(2)
    q = q_ref[...]
    q0 = q_off + i * tq
    n_full = jnp.minimum(((q0 // CHUNK + 1) * CHUNK) // tk, s_len // tk)
    n_kv = jnp.minimum((((q0 + tq - 1) // CHUNK + 1) * CHUNK + tk - 1) // tk, s_len // tk)

    m_ref[...] = jnp.full(m_ref.shape, NEG_INF, F32)
    l_ref[...] = jnp.zeros(l_ref.shape, F32)
    acc_ref[...] = jnp.zeros(acc_ref.shape, F32)

    def step(j, masked):
        ks = pl.ds(pl.multiple_of(j * tk, tk), tk)
        k = jnp.concatenate([k_ref[ks, :], kr_ref[ks, :]], axis=1)
        s = lax.dot_general(q, k, (((1,), (1,)), ((), ())), preferred_element_type=F32)
        if masked:
            qc = (q0 + lax.broadcasted_iota(jnp.int32, (tq, 1), 0)) // CHUNK
            kc = (j * tk + lax.broadcasted_iota(jnp.int32, (1, tk), 1)) // CHUNK
            s = jnp.where(kc <= qc, s, NEG_INF)
        m_prev = m_ref[...]
        m_next = jnp.maximum(m_prev, jnp.max(s, axis=1, keepdims=True))
        alpha = jnp.exp(m_prev - m_next)
        p = jnp.exp(s - _lane_tile(m_next, tk // LANES))
        l_ref[...] = alpha * l_ref[...] + jnp.sum(p, axis=1, keepdims=True)
        acc_ref[...] = alpha * acc_ref[...] + jnp.dot(p.astype(BF16), v_ref[ks, :],
                                                      preferred_element_type=F32)
        m_ref[...] = m_next

    lax.fori_loop(0, n_full, lambda j, c: (step(j, False), c)[1], 0)
    lax.fori_loop(n_full, n_kv, lambda j, c: (step(j, True), c)[1], 0)
    o_ref[...] = acc_ref[...] / l_ref[...]


def _sb_kernel(q_ref, k_ref, v_ref, u_ref, o_ref, r_ref, acc_ref, *, tq, tk, q_off, s_len):
    i = pl.program_id(2)
    q = q_ref[...]
    q0 = q_off + i * tq
    n_full = jnp.minimum(q0 // tk, s_len // tk)
    n_kv = jnp.minimum((q0 + tq - 1 + tk - 1) // tk, s_len // tk)
    u = u_ref[...]

    r_ref[...] = jnp.zeros(r_ref.shape, F32)
    acc_ref[...] = jnp.zeros(acc_ref.shape, F32)

    def step(j, masked):
        ks = pl.ds(pl.multiple_of(j * tk, tk), tk)
        z = lax.dot_general(q, k_ref[ks, :], (((1,), (1,)), ((), ())), preferred_element_type=F32)
        softplus = jnp.maximum(z, 0.0) + jnp.log(1.0 + jnp.exp(-jnp.abs(z)))
        log_fail = -softplus
        log_beta = z - softplus
        if masked:
            qp = q0 + lax.broadcasted_iota(jnp.int32, (tq, 1), 0)
            kp = j * tk + lax.broadcasted_iota(jnp.int32, (1, tk), 1)
            before = kp < qp
            log_fail = jnp.where(before, log_fail, 0.0)
        hi = log_fail.astype(BF16)
        lo = (log_fail - hi.astype(F32)).astype(BF16)
        later = (jnp.dot(hi, u, preferred_element_type=F32)
                 + jnp.dot(lo, u, preferred_element_type=F32))
        r_prev = r_ref[...]
        w = jnp.exp(log_beta + later + _lane_tile(r_prev, tk // LANES))
        if masked:
            w = jnp.where(before, w, 0.0)
        acc_ref[...] += jnp.dot(w.astype(BF16), v_ref[ks, :], preferred_element_type=F32)
        r_ref[...] = r_prev + jnp.sum(log_fail, axis=1, keepdims=True)

    lax.fori_loop(0, n_kv - n_full, lambda t, c: (step(n_kv - 1 - t, True), c)[1], 0)
    lax.fori_loop(0, n_full, lambda t, c: (step(n_full - 1 - t, False), c)[1], 0)
    o_ref[...] = acc_ref[...]


def _attention(kernel, q, kv_args, extra, *, batch, heads, t_len, s_len, q_off, tq, tk, scratch, name):
    tq = min(tq, t_len)
    nq = t_len // tq
    dq = q.shape[-1]
    in_specs = [pl.BlockSpec((None, tq, dq), lambda b, h, i: (h, b * nq + i, 0))]
    args = [q]
    for a, per_head in kv_args:
        if per_head:
            in_specs.append(pl.BlockSpec((None, s_len, a.shape[-1]), lambda b, h, i: (h, b, 0)))
        else:
            in_specs.append(pl.BlockSpec((s_len, a.shape[-1]), lambda b, h, i: (b, 0)))
        args.append(a)
    for a in extra:
        in_specs.append(pl.BlockSpec(a.shape, lambda b, h, i: (0,) * a.ndim))
        args.append(a)
    return pl.pallas_call(
        functools.partial(kernel, tq=tq, tk=tk, q_off=q_off, s_len=s_len),
        grid=(batch, heads, nq),
        in_specs=in_specs,
        out_specs=pl.BlockSpec((tq, LANES), lambda b, h, i: (b * nq + i, h)),
        out_shape=jax.ShapeDtypeStruct((batch * t_len, heads * LANES), F32),
        scratch_shapes=scratch(tq),
        compiler_params=_params("parallel", "parallel", "arbitrary"),
        name=name,
    )(*args)


def mla_attention(q, k, kr, v, **kw):
    scratch = lambda tq: [pltpu.VMEM((tq, LANES), F32)] * 3
    return _attention(_mla_kernel, q, [(k, True), (kr, False), (v, True)], [], scratch=scratch,
                      name="mla_attention", **kw)


def sb_attention(q, k, v, **kw):
    tk = kw["tk"]
    u = (lax.broadcasted_iota(jnp.int32, (tk, tk), 0)
         > lax.broadcasted_iota(jnp.int32, (tk, tk), 1)).astype(BF16)
    scratch = lambda tq: [pltpu.VMEM((tq, LANES), F32)] * 2
    return _attention(_sb_kernel, q, [(k, True), (v, True)], [u], scratch=scratch,
                      name="sb_attention", **kw)


def _rope_table(pos, rope):
    half = rope // 2
    inv_freq = jnp.power(ROPE_THETA, -jnp.arange(half, dtype=F32) / half)
    ang = pos.astype(F32)[:, None] * inv_freq[None, :]
    cos, sin = jnp.cos(ang), jnp.sin(ang)
    return jnp.concatenate([cos, cos, -sin, sin], axis=1)


def _swap_halves(a, axis=-1):
    lo, hi = jnp.split(a, 2, axis=axis)
    return jnp.concatenate([hi, lo], axis=axis)


def _prep_layer(l, dims, w_in, w_q_b, w_kv_b, w_o, w_up, w_down, g_q_nope, g_q_rope, g_k_rope):
    ql, kvl, rope, nope, vdim, heads, sbw = (dims[k] for k in
                                             ("ql", "kvl", "rope", "nope", "vdim", "heads", "sbw"))
    wi = w_in[l]
    d = wi.shape[0]
    kr = wi[:, ql + kvl:ql + kvl + rope]
    lat_w = ql + kvl + 2 * rope
    pad = (-lat_w) % MXU_DIM
    w_lat = jnp.concatenate([wi[:, :ql + kvl], kr, _swap_halves(kr), jnp.zeros((d, pad), F32)], axis=1)
    o = ql + kvl + rope
    w_sq, w_sk, w_sv = (wi[:, o + n * sbw:o + (n + 1) * sbw] for n in range(3))

    wq = w_q_b[l].reshape(ql, heads, nope + rope)
    wq_r = wq[:, :, nope:]
    wq = jnp.concatenate([wq[:, :, :nope], wq_r, _swap_halves(wq_r)], axis=2).reshape(ql, heads * 2 * LANES)

    wkv = w_kv_b[l].reshape(kvl, heads, nope + vdim)
    wk = wkv[:, :, :nope].reshape(kvl, heads * nope)
    wv = wkv[:, :, nope:].reshape(kvl, heads * vdim)

    scale = (nope + rope) ** -0.5
    qrow = (jnp.concatenate([g_q_nope[l], g_q_rope[l], _swap_halves(g_q_rope[l])]) * scale).reshape(1, -1)
    krow = jnp.concatenate([g_k_rope[l], _swap_halves(g_k_rope[l])]).reshape(1, -1)
    bf = lambda a: a.astype(BF16)
    return dict(w_lat=bf(w_lat), w_sq=bf(w_sq * (sbw // dims["sbh"]) ** -0.5), w_sk=bf(w_sk), w_sv=bf(w_sv),
                wq=bf(wq), wk=bf(wk), wv=bf(wv), w_o=bf(w_o[l]), w_up=bf(w_up[l]), w_down=bf(w_down[l]),
                qrow=qrow, krow=krow)


def _layer(x, past, tab, lw, gains, dims, *, batch, t_len, q_off, tq_mla, tk_mla, tq_sb, tk_sb, tm_lat):
    ql, kvl, rope, heads, sbh = (dims[k] for k in ("ql", "kvl", "rope", "heads", "sbh"))
    m = x.shape[0]
    xn = norm_cast([x], [gains["g_attn"]])
    (sq,) = matmul(xn, lw["w_sq"], [(BF16, "head")])
    sk, skb = matmul(xn, lw["w_sk"], [(F32, "tok"), (BF16, "head")])
    sv, svb = matmul(xn, lw["w_sv"], [(F32, "tok"), (BF16, "head")])
    (lat,) = matmul(xn, lw["w_lat"], [(F32, "tok")])
    q_mla, ckv, ckvb, krope, kropeb = latent_post(
        lat, gains["g_q_a"].reshape(1, -1), lw["wq"], lw["qrow"], gains["g_kv_a"].reshape(1, -1),
        lw["krow"], tab, ql=ql, kvl=kvl, heads=heads, rope=rope, tm=tm_lat)

    if past is None:
        s_len = t_len
        c_all, kr_all, sbk_all, sbv_all = ckvb, kropeb, skb, svb
    else:
        p_ckv, p_kr, p_sbk, p_sbv = past
        past_len = p_ckv.shape[1]
        s_len = -(-(past_len + t_len) // MXU_DIM) * MXU_DIM
        fill = s_len - past_len - t_len

        def rows(cached, new, width):
            new = new.reshape(batch, t_len, width).astype(BF16)
            cached = cached.reshape(batch, past_len, width).astype(BF16)
            return jnp.concatenate([cached, new, jnp.zeros((batch, fill, width), BF16)], axis=1)

        def head_major(a, width):
            return a.reshape(batch * s_len, sbh, width // sbh).transpose(1, 0, 2)

        c_all = rows(p_ckv, ckv, kvl).reshape(batch * s_len, kvl)
        kr_all = jnp.pad(rows(p_kr, krope, rope), ((0, 0), (0, 0), (0, LANES - rope))).reshape(batch * s_len, LANES)
        sbw = sk.shape[1]
        sbk_all = head_major(rows(p_sbk, sk, sbw), sbw)
        sbv_all = head_major(rows(p_sbv, sv, sbw), sbw)

    k_mla, v_mla = kv_expand(c_all, lw["wk"], lw["wv"], gains["g_k_nope"].reshape(1, -1), heads=heads)
    common = dict(batch=batch, t_len=t_len, s_len=s_len, q_off=q_off)
    mla_out = mla_attention(q_mla, k_mla, kr_all, v_mla, heads=heads, tq=tq_mla, tk=tk_mla, **common)
    sb_out = sb_attention(sq, sbk_all, sbv_all, heads=sbh, tq=tq_sb, tk=tk_sb, **common)

    merged = norm_cast([mla_out, sb_out], [gains["g_out_mla"], gains["g_out_sb"]])
    (h,) = matmul(merged, lw["w_o"], [(F32, "tok")], res=x, tn=512)
    hn = norm_cast([h], [gains["g_mlp"]])
    (u,) = matmul(hn, lw["w_up"], [(BF16, "tok")], act="relu2")
    (y,) = matmul(u, lw["w_down"], [(F32, "tok")], res=h, tk=2048)
    return y, (ckv, krope, sk, sv)


def kernel(x_prompt, x_sample, cache_mla_ckv, cache_mla_krope, cache_sb_k, cache_sb_v,
           g_attn, w_in, g_q_a, w_q_b, g_kv_a, w_kv_b, g_q_nope, g_q_rope, g_k_nope, g_k_rope,
           g_out_mla, g_out_sb, w_o, g_mlp, w_up, w_down):
    depth = w_in.shape[0]
    bp, tp, d = x_prompt.shape
    bs, ts, _ = x_sample.shape
    past_len = cache_mla_ckv.shape[2]
    sbh, sbd = cache_sb_k.shape[-2:]
    nope, rope = g_q_nope.shape[-1], g_q_rope.shape[-1]
    ql, kvl = g_q_a.shape[-1], g_kv_a.shape[-1]
    heads = w_q_b.shape[-1] // (nope + rope)
    vdim = w_kv_b.shape[-1] // heads - nope
    assert nope == LANES and vdim == LANES and sbd == LANES and 2 * rope == LANES
    dims = dict(ql=ql, kvl=kvl, rope=rope, nope=nope, vdim=vdim, heads=heads, sbh=sbh, sbw=sbh * sbd)

    tab_p = _rope_table(jnp.arange(tp, dtype=jnp.int32), rope)
    tab_s = _rope_table(past_len + jnp.arange(ts, dtype=jnp.int32), rope)

    hp = x_prompt.reshape(bp * tp, d)
    hs = x_sample.reshape(bs * ts, d)
    rows_p, rows_s = [], []
    for l in range(depth):
        lw = _prep_layer(l, dims, w_in, w_q_b, w_kv_b, w_o, w_up, w_down, g_q_nope, g_q_rope, g_k_rope)
        gains = dict(g_attn=g_attn[l], g_q_a=g_q_a[l], g_kv_a=g_kv_a[l], g_k_nope=g_k_nope[l],
                     g_out_mla=g_out_mla[l], g_out_sb=g_out_sb[l], g_mlp=g_mlp[l])
        hp, new_p = _layer(hp, None, tab_p, lw, gains, dims, batch=bp, t_len=tp, q_off=0,
                           tq_mla=512, tk_mla=512, tq_sb=512, tk_sb=256, tm_lat=256)
        past = (cache_mla_ckv[l], cache_mla_krope[l], cache_sb_k[l], cache_sb_v[l])
        hs, new_s = _layer(hs, past, tab_s, lw, gains, dims, batch=bs, t_len=ts, q_off=past_len,
                           tq_mla=ts, tk_mla=MXU_DIM, tq_sb=ts, tk_sb=MXU_DIM, tm_lat=ts)
        rows_p.append(new_p)
        rows_s.append(new_s)

    def stack(rows, n, b, t, shape):
        return jnp.stack([r[n] for r in rows]).reshape(depth, b, t, *shape)

    return (hp.reshape(bp, tp, d), hs.reshape(bs, ts, d),
            stack(rows_p, 0, bp, tp, (kvl,)), stack(rows_p, 1, bp, tp, (rope,)),
            stack(rows_p, 2, bp, tp, (sbh, sbd)), stack(rows_p, 3, bp, tp, (sbh, sbd)),
            stack(rows_s, 0, bs, ts, (kvl,)), stack(rows_s, 1, bs, ts, (rope,)),
            stack(rows_s, 2, bs, ts, (sbh, sbd)), stack(rows_s, 3, bs, ts, (sbh, sbd)))
```

```python
import functools

import jax
import jax.numpy as jnp
from jax import lax
from jax.experimental import pallas as pl
from jax.experimental.pallas import tpu as pltpu

EPS = 1e-6
NEG_INF = -1e30
CHUNK = 64
ROPE_THETA = 10000.0
LANES = 128
MXU_DIM = 256
VMEM_LIMIT = 56 * 1024 * 1024

F32 = jnp.float32
BF16 = jnp.bfloat16


def _pick(n, cap, mult=LANES):
    if n <= cap:
        return n
    best = None
    for d in range(mult, cap + 1, mult):
        if n % d == 0:
            best = d
    assert best is not None, (n, cap, mult)
    return best


def _params(*sem):
    return pltpu.CompilerParams(dimension_semantics=sem, vmem_limit_bytes=VMEM_LIMIT)


def _rms(x, g):
    return x * lax.rsqrt(jnp.mean(x * x, axis=-1, keepdims=True) + EPS) * g


def _norm_cast_kernel(*refs, n_in):
    o_ref = refs[2 * n_in]
    off = 0
    for x_ref, g_ref in zip(refs[:n_in], refs[n_in:2 * n_in]):
        w = x_ref.shape[1]
        o_ref[:, off:off + w] = _rms(x_ref[...], g_ref[...]).astype(o_ref.dtype)
        off += w


def norm_cast(xs, gs, tm=256):
    m = xs[0].shape[0]
    tm = _pick(m, tm, 8)
    widths = [x.shape[1] for x in xs]
    n = len(xs)
    return pl.pallas_call(
        functools.partial(_norm_cast_kernel, n_in=n),
        grid=(m // tm,),
        in_specs=[pl.BlockSpec((tm, w), lambda i: (i, 0)) for w in widths]
        + [pl.BlockSpec((1, w), lambda i: (0, 0)) for w in widths],
        out_specs=pl.BlockSpec((tm, sum(widths)), lambda i: (i, 0)),
        out_shape=jax.ShapeDtypeStruct((m, sum(widths)), BF16),
        compiler_params=_params("parallel"),
        name="norm_cast",
    )(*xs, *[g.reshape(1, -1) for g in gs])


def _mm_kernel(*refs, nk, act, has_res, n_aliased, layouts):
    x_ref, w_ref = refs[0], refs[1]
    p = 2
    res_ref = None
    if has_res:
        res_ref = refs[p]
        p += 1
    p += n_aliased
    out_refs = refs[p:p + len(layouts)]
    acc_ref = refs[p + len(layouts)] if nk > 1 else None

    def finish(r):
        if act == "relu2":
            r = jnp.maximum(r, 0.0)
            r = r * r
        if has_res:
            r = r + res_ref[...]
        for o_ref, layout in zip(out_refs, layouts):
            if layout == "tok":
                o_ref[...] = r.astype(o_ref.dtype)
            else:
                for hh in range(o_ref.shape[0]):
                    o_ref[hh] = r[:, hh * LANES:(hh + 1) * LANES].astype(o_ref.dtype)

    part = jnp.dot(x_ref[...], w_ref[...], preferred_element_type=F32)
    if nk == 1:
        finish(part)
    else:
        k = pl.program_id(2)

        @pl.when(k == 0)
        def _():
            acc_ref[...] = part

        @pl.when(jnp.logical_and(k > 0, k < nk - 1))
        def _():
            acc_ref[...] += part

        @pl.when(k == nk - 1)
        def _():
            finish(acc_ref[...] + part)


class Stack:
    def __init__(self, prev, layer, depth):
        self.prev, self.layer, self.depth = prev, layer, depth


def matmul(x, w, outs, act=None, res=None, tm=1024, tn=1024, tk=None, stack=None):
    m, kdim = x.shape
    n = w.shape[1]
    tm = _pick(m, tm, 8)
    tn = _pick(n, tn)
    tk = kdim if tk is None else _pick(kdim, tk)
    nk = kdim // tk
    grid = (m // tm, n // tn, nk)
    in_specs = [pl.BlockSpec((tm, tk), lambda i, j, k: (i, k)),
                pl.BlockSpec((tk, tn), lambda i, j, k: (k, j))]
    args = [x, w]
    if res is not None:
        in_specs.append(pl.BlockSpec((tm, tn), lambda i, j, k: (i, j)))
        args.append(res)
    aliases = {}
    if stack is not None and stack.prev is not None:
        aliases[len(args)] = 0
        in_specs.append(pl.BlockSpec(memory_space=pl.ANY))
        args.append(stack.prev)
    out_specs, out_shapes = [], []
    for o, (dtype, layout) in enumerate(outs):
        if layout == "tok" and o == 0 and stack is not None:
            layer = stack.layer
            out_specs.append(pl.BlockSpec((None, tm, tn), lambda i, j, k: (layer, i, j)))
            out_shapes.append(jax.ShapeDtypeStruct((stack.depth, m, n), dtype))
        elif layout == "tok":
            out_specs.append(pl.BlockSpec((tm, tn), lambda i, j, k: (i, j)))
            out_shapes.append(jax.ShapeDtypeStruct((m, n), dtype))
        else:
            out_specs.append(pl.BlockSpec((tn // LANES, tm, LANES), lambda i, j, k: (j, i, 0)))
            out_shapes.append(jax.ShapeDtypeStruct((n // LANES, m, LANES), dtype))
    scratch = [pltpu.VMEM((tm, tn), F32)] if nk > 1 else []
    return pl.pallas_call(
        functools.partial(_mm_kernel, nk=nk, act=act, has_res=res is not None, n_aliased=len(aliases),
                          layouts=tuple(l for _, l in outs)),
        grid=grid,
        in_specs=in_specs,
        out_specs=out_specs,
        out_shape=out_shapes,
        scratch_shapes=scratch,
        input_output_aliases=aliases,
        compiler_params=_params("parallel", "parallel", "arbitrary"),
        name="matmul",
    )(*args)


def _latent_kernel(lat_ref, gqa_ref, wq_ref, qrow_ref, gkva_ref, krow_ref, tab_ref, *refs,
                   ql, kvl, heads, rope):
    q_ref, ckv_ref, ckvb_ref, kr_ref, krb_ref = refs[-5:]
    lat = lat_ref[...]
    tab = tab_ref[...]

    def rotate(x, row):
        t = _rms(x, row) * tab
        return t + pltpu.roll(t, rope, axis=1)

    qn = _rms(lat[:, :ql], gqa_ref[...]).astype(BF16)
    q = jnp.dot(qn, wq_ref[...], preferred_element_type=F32)
    qrow = qrow_ref[...]
    for h in range(heads):
        base = h * 2 * LANES
        nope = _rms(q[:, base:base + LANES], qrow[:, :LANES])
        q_ref[h, :, :LANES] = nope.astype(BF16)
        q_ref[h, :, LANES:] = rotate(q[:, base + LANES:base + 2 * LANES], qrow[:, LANES:]).astype(BF16)

    ckv = _rms(lat[:, ql:ql + kvl], gkva_ref[...])
    ckv_ref[...] = ckv
    ckvb_ref[...] = ckv.astype(BF16)

    kr = rotate(lat[:, ql + kvl:ql + kvl + LANES], krow_ref[...])
    kr_ref[...] = kr[:, :rope]
    lane = lax.broadcasted_iota(jnp.int32, kr.shape, 1)
    krb_ref[...] = jnp.where(lane < rope, kr, 0.0).astype(BF16)


def latent_post(lat, gqa, wq, qrow, gkva, krow, tab, *, ql, kvl, heads, rope, tm, stack_ckv, stack_kr):
    m = lat.shape[0]
    t = tab.shape[0]
    tm = min(tm, t)
    assert t % tm == 0 and m % tm == 0
    nt = t // tm
    layer, depth = stack_ckv.layer, stack_ckv.depth
    full = lambda a: pl.BlockSpec(a.shape, lambda i: (0,) * a.ndim)
    args = [lat, gqa, wq, qrow, gkva, krow, tab]
    in_specs = [pl.BlockSpec((tm, lat.shape[1]), lambda i: (i, 0)),
                full(gqa), full(wq), full(qrow), full(gkva), full(krow),
                pl.BlockSpec((tm, LANES), lambda i: (i % nt, 0))]
    aliases = {}
    for prev, out_idx in ((stack_ckv.prev, 1), (stack_kr.prev, 3)):
        if prev is not None:
            aliases[len(args)] = out_idx
            in_specs.append(pl.BlockSpec(memory_space=pl.ANY))
            args.append(prev)
    return pl.pallas_call(
        functools.partial(_latent_kernel, ql=ql, kvl=kvl, heads=heads, rope=rope),
        grid=(m // tm,),
        in_specs=in_specs,
        out_specs=[pl.BlockSpec((heads, tm, 2 * LANES), lambda i: (0, i, 0)),
                   pl.BlockSpec((None, tm, kvl), lambda i: (layer, i, 0)),
                   pl.BlockSpec((tm, kvl), lambda i: (i, 0)),
                   pl.BlockSpec((None, tm, rope), lambda i: (layer, i, 0)),
                   pl.BlockSpec((tm, LANES), lambda i: (i, 0))],
        out_shape=[jax.ShapeDtypeStruct((heads, m, 2 * LANES), BF16),
                   jax.ShapeDtypeStruct((depth, m, kvl), F32),
                   jax.ShapeDtypeStruct((m, kvl), BF16),
                   jax.ShapeDtypeStruct((depth, m, rope), F32),
                   jax.ShapeDtypeStruct((m, LANES), BF16)],
        input_output_aliases=aliases,
        compiler_params=_params("parallel"),
        name="latent_post",
    )(*args)


def _kv_expand_kernel(c_ref, wk_ref, wv_ref, g_ref, k_ref, v_ref, *, heads):
    c = c_ref[...]
    k = jnp.dot(c, wk_ref[...], preferred_element_type=F32)
    v = jnp.dot(c, wv_ref[...], preferred_element_type=F32)
    g = g_ref[...]
    for h in range(heads):
        sl = slice(h * LANES, (h + 1) * LANES)
        k_ref[h] = _rms(k[:, sl], g).astype(BF16)
        v_ref[h] = v[:, sl].astype(BF16)


def kv_expand(c, wk, wv, g, *, heads, tm=512):
    rows = c.shape[0]
    tm = _pick(rows, tm, 16)
    full = lambda a: pl.BlockSpec(a.shape, lambda i: (0,) * a.ndim)
    hm = jax.ShapeDtypeStruct((heads, rows, LANES), BF16)
    return pl.pallas_call(
        functools.partial(_kv_expand_kernel, heads=heads),
        grid=(rows // tm,),
        in_specs=[pl.BlockSpec((tm, c.shape[1]), lambda i: (i, 0)), full(wk), full(wv), full(g)],
        out_specs=[pl.BlockSpec((heads, tm, LANES), lambda i: (0, i, 0))] * 2,
        out_shape=[hm, hm],
        compiler_params=_params("parallel"),
        name="kv_expand",
    )(c, wk, wv, g)


def _lane_tile(x, reps):
    return x if reps == 1 else jnp.concatenate([x] * reps, axis=1)


LOG2E = 1.4426950408889634
SB_LOG2_ZERO = -160.0


def _side_effect_loop(lo, hi, fn):
    lax.fori_loop(lo, hi, lambda j, c: (fn(j), c)[1], 0)


def _mla_kernel(q_ref, k_ref, kr_ref, v_ref, o_ref, s_ref, p_ref, m_ref, l_ref, a_ref, acc_ref,
                *, hg, tq, tk, rc, q_off, t_len, s_len):
    nkb = s_len // tk
    heads = range(hg)

    def q_block(qi):
        rows = pl.ds(pl.multiple_of(qi * tq, tq), tq)
        q0 = q_off + pl.program_id(2) * t_len + qi * tq
        n_full = jnp.minimum(((q0 // CHUNK + 1) * CHUNK) // tk, nkb)
        n_kv = jnp.minimum((((q0 + tq - 1) // CHUNK + 1) * CHUNK + tk - 1) // tk, nkb)
        m_ref[...] = jnp.full(m_ref.shape, NEG_INF, F32)
        l_ref[...] = jnp.zeros(l_ref.shape, F32)
        acc_ref[...] = jnp.zeros(acc_ref.shape, F32)

        def step(j, masked):
            ks = pl.ds(pl.multiple_of(j * tk, tk), tk)
            kr = kr_ref[ks, :]
            for g in heads:
                k = jnp.concatenate([k_ref[g, ks, :], kr], axis=1)
                s_ref[g] = lax.dot_general(q_ref[g, rows, :], k, (((1,), (1,)), ((), ())),
                                           preferred_element_type=F32)
            for g in heads:
                for c in range(tq // rc):
                    rs = slice(c * rc, (c + 1) * rc)
                    s = s_ref[g, rs, :]
                    if masked:
                        qc = (q0 + c * rc + lax.broadcasted_iota(jnp.int32, (rc, 1), 0)) // CHUNK
                        kc = (j * tk + lax.broadcasted_iota(jnp.int32, (1, tk), 1)) // CHUNK
                        s = jnp.where(kc <= qc, s, NEG_INF)
                    m_prev = m_ref[g, rs, :]
                    m_next = jnp.maximum(m_prev, jnp.max(s, axis=1, keepdims=True))
                    alpha = jnp.exp2(m_prev - m_next)
                    p = jnp.exp2(s - _lane_tile(m_next, tk // LANES))
                    l_ref[g, rs, :] = alpha * l_ref[g, rs, :] + jnp.sum(p, axis=1, keepdims=True)
                    m_ref[g, rs, :] = m_next
                    a_ref[g, rs, :] = alpha
                    p_ref[g, rs, :] = p.astype(BF16)
            for g in heads:
                acc_ref[g] = a_ref[g] * acc_ref[g] + jnp.dot(p_ref[g], v_ref[g, ks, :],
                                                             preferred_element_type=F32)

        _side_effect_loop(0, n_full, lambda j: step(j, False))
        _side_effect_loop(n_full, n_kv, lambda j: step(j, True))
        for g in heads:
            o_ref[rows, g * LANES:(g + 1) * LANES] = acc_ref[g] / l_ref[g]

    _side_effect_loop(0, t_len // tq, q_block)


def _sb_kernel(q_ref, k_ref, v_ref, u_ref, o_ref, z_ref, later_ref, hi_ref, lo_ref, r_ref, rsum_ref,
               acc_ref, *, hg, tq, tk, rc, q_off, t_len, s_len):
    nkb = s_len // tk
    reps = tk // LANES
    heads = range(hg)

    def q_block(qi):
        rows = pl.ds(pl.multiple_of(qi * tq, tq), tq)
        q0 = q_off + pl.program_id(2) * t_len + qi * tq
        n_full = jnp.minimum(q0 // tk, nkb)
        n_kv = jnp.minimum((q0 + tq - 1 + tk - 1) // tk, nkb)
        r_ref[...] = jnp.zeros(r_ref.shape, F32)
        acc_ref[...] = jnp.zeros(acc_ref.shape, F32)

        def step(j, masked):
            ks = pl.ds(pl.multiple_of(j * tk, tk), tk)

            def before(c):
                qp = q0 + c * rc + lax.broadcasted_iota(jnp.int32, (rc, 1), 0)
                kp = j * tk + lax.broadcasted_iota(jnp.int32, (1, tk), 1)
                return kp < qp

            for g in heads:
                z_ref[g] = lax.dot_general(q_ref[g, rows, :], k_ref[g, ks, :], (((1,), (1,)), ((), ())),
                                           preferred_element_type=F32)
            for g in heads:
                for c in range(tq // rc):
                    rs = slice(c * rc, (c + 1) * rc)
                    z = z_ref[g, rs, :]
                    fail = jnp.maximum(z, 0.0) + jnp.log(1.0 + jnp.exp2(-jnp.abs(z))) * LOG2E
                    z_ref[g, rs, :] = z - fail
                    if masked:
                        fail = jnp.where(before(c), fail, 0.0)
                    hi = fail.astype(BF16)
                    hi_ref[g, rs, :] = hi
                    lo_ref[g, rs, :] = (fail - hi.astype(F32)).astype(BF16)
                    rsum_ref[g, rs, :] = jnp.broadcast_to(jnp.sum(fail, axis=1, keepdims=True), (rc, LANES))
            u = u_ref[...]
            for g in heads:
                later_ref[g] = (jnp.dot(hi_ref[g], u, preferred_element_type=F32)
                                + jnp.dot(lo_ref[g], u, preferred_element_type=F32))
            for g in heads:
                for c in range(tq // rc):
                    rs = slice(c * rc, (c + 1) * rc)
                    r_prev = r_ref[g, rs, :]
                    w = jnp.exp2(z_ref[g, rs, :] - later_ref[g, rs, :] - _lane_tile(r_prev, reps))
                    if masked:
                        w = jnp.where(before(c), w, 0.0)
                    hi_ref[g, rs, :] = w.astype(BF16)
                    r_ref[g, rs, :] = r_prev + rsum_ref[g, rs, :]
            for g in heads:
                acc_ref[g] += jnp.dot(hi_ref[g], v_ref[g, ks, :], preferred_element_type=F32)

        _side_effect_loop(0, n_kv - n_full, lambda t: step(n_kv - 1 - t, True))

        def more(c):
            return jnp.logical_and(c[0] >= 0, c[1] < -SB_LOG2_ZERO)

        def visit(c):
            step(c[0], False)
            return c[0] - 1, jnp.min(r_ref[...])

        lax.while_loop(more, visit, (n_full - 1, jnp.min(r_ref[...])))
        for g in heads:
            o_ref[rows, g * LANES:(g + 1) * LANES] = acc_ref[g]

    _side_effect_loop(0, t_len // tq, q_block)


def _attention(kernel, q, kv_args, extra, *, batch, heads, hg, t_len, s_len, q_off, tq, tk, rc, scratch, name,
               span=1024):
    tq = min(tq, t_len)
    rc = min(rc, tq)
    span = min(span, t_len)
    nspan = t_len // span
    assert t_len % span == 0 and span % tq == 0 and tq % rc == 0 and s_len % tk == 0 and heads % hg == 0
    in_specs = [pl.BlockSpec((hg, span, q.shape[-1]), lambda b, h, t: (h, b * nspan + t, 0))]
    args = [q]
    for a, per_head in kv_args:
        if per_head:
            in_specs.append(pl.BlockSpec((hg, s_len, a.shape[-1]), lambda b, h, t: (h, b, 0)))
        else:
            in_specs.append(pl.BlockSpec((s_len, a.shape[-1]), lambda b, h, t: (b, 0)))
        args.append(a)
    for a in extra:
        in_specs.append(pl.BlockSpec(a.shape, lambda b, h, t: (0,) * a.ndim))
        args.append(a)
    return pl.pallas_call(
        functools.partial(kernel, hg=hg, tq=tq, tk=tk, rc=rc, q_off=q_off, t_len=span, s_len=s_len),
        grid=(batch, heads // hg, nspan),
        in_specs=in_specs,
        out_specs=pl.BlockSpec((span, hg * LANES), lambda b, h, t: (b * nspan + t, h)),
        out_shape=jax.ShapeDtypeStruct((batch * t_len, heads * LANES), F32),
        scratch_shapes=scratch(hg, tq, tk),
        compiler_params=_params("parallel", "parallel", "parallel"),
        name=name,
    )(*args)


def mla_attention(q, k, kr, v, **kw):
    scratch = lambda hg, tq, tk: ([pltpu.VMEM((hg, tq, tk), F32), pltpu.VMEM((hg, tq, tk), BF16)]
                                  + [pltpu.VMEM((hg, tq, LANES), F32)] * 4)
    return _attention(_mla_kernel, q, [(k, True), (kr, False), (v, True)], [], scratch=scratch,
                      name="mla_attention", **kw)


def sb_attention(q, k, v, **kw):
    tk = kw["tk"]
    u = (lax.broadcasted_iota(jnp.int32, (tk, tk), 0)
         > lax.broadcasted_iota(jnp.int32, (tk, tk), 1)).astype(BF16)
    scratch = lambda hg, tq, tk: ([pltpu.VMEM((hg, tq, tk), F32)] * 2 + [pltpu.VMEM((hg, tq, tk), BF16)] * 2
                                  + [pltpu.VMEM((hg, tq, LANES), F32)] * 3)
    return _attention(_sb_kernel, q, [(k, True), (v, True)], [u], scratch=scratch,
                      name="sb_attention", **kw)


def _rope_table(pos, rope):
    half = rope // 2
    inv_freq = jnp.power(ROPE_THETA, -jnp.arange(half, dtype=F32) / half)
    ang = pos.astype(F32)[:, None] * inv_freq[None, :]
    cos, sin = jnp.cos(ang), jnp.sin(ang)
    return jnp.concatenate([cos, cos, -sin, sin], axis=1)


def _swap_halves(a, axis=-1):
    lo, hi = jnp.split(a, 2, axis=axis)
    return jnp.concatenate([hi, lo], axis=axis)


def _prep_layer(l, dims, w_in, w_q_b, w_kv_b, w_o, w_up, w_down, g_q_nope, g_q_rope, g_k_rope):
    ql, kvl, rope, nope, vdim, heads, sbw = (dims[k] for k in
                                             ("ql", "kvl", "rope", "nope", "vdim", "heads", "sbw"))
    wi = w_in[l]
    d = wi.shape[0]
    kr = wi[:, ql + kvl:ql + kvl + rope]
    lat_w = ql + kvl + 2 * rope
    pad = (-lat_w) % MXU_DIM
    w_lat = jnp.concatenate([wi[:, :ql + kvl], kr, _swap_halves(kr), jnp.zeros((d, pad), F32)], axis=1)
    o = ql + kvl + rope
    w_sq, w_sk, w_sv = (wi[:, o + n * sbw:o + (n + 1) * sbw] for n in range(3))

    wq = w_q_b[l].reshape(ql, heads, nope + rope)
    wq_r = wq[:, :, nope:]
    wq = jnp.concatenate([wq[:, :, :nope], wq_r, _swap_halves(wq_r)], axis=2).reshape(ql, heads * 2 * LANES)

    wkv = w_kv_b[l].reshape(kvl, heads, nope + vdim)
    wk = wkv[:, :, :nope].reshape(kvl, heads * nope)
    wv = wkv[:, :, nope:].reshape(kvl, heads * vdim)

    scale = LOG2E * (nope + rope) ** -0.5
    qrow = (jnp.concatenate([g_q_nope[l], g_q_rope[l], _swap_halves(g_q_rope[l])]) * scale).reshape(1, -1)
    krow = jnp.concatenate([g_k_rope[l], _swap_halves(g_k_rope[l])]).reshape(1, -1)
    bf = lambda a: a.astype(BF16)
    sb_scale = LOG2E * (sbw // dims["sbh"]) ** -0.5
    return dict(w_lat=bf(w_lat), w_sq=bf(w_sq * sb_scale), w_sk=bf(w_sk), w_sv=bf(w_sv),
                wq=bf(wq), wk=bf(wk), wv=bf(wv), w_o=bf(w_o[l]), w_up=bf(w_up[l]), w_down=bf(w_down[l]),
                qrow=qrow, krow=krow)


def _layer(x, past, tab, lw, gains, dims, leaves, layer, depth, *, batch, t_len, q_off, tq_mla, tk_mla,
           tq_sb, tk_sb, tm_lat):
    ql, kvl, rope, heads, sbh = (dims[k] for k in ("ql", "kvl", "rope", "heads", "sbh"))
    prev = leaves if leaves is not None else (None,) * 4
    slot = lambda n: Stack(prev[n], layer, depth)
    xn = norm_cast([x], [gains["g_attn"]])
    (sq,) = matmul(xn, lw["w_sq"], [(BF16, "head")])
    sk, skb = matmul(xn, lw["w_sk"], [(F32, "tok"), (BF16, "head")], stack=slot(2))
    sv, svb = matmul(xn, lw["w_sv"], [(F32, "tok"), (BF16, "head")], stack=slot(3))
    (lat,) = matmul(xn, lw["w_lat"], [(F32, "tok")])
    q_mla, ckv, ckvb, krope, kropeb = latent_post(
        lat, gains["g_q_a"].reshape(1, -1), lw["wq"], lw["qrow"], gains["g_kv_a"].reshape(1, -1),
        lw["krow"], tab, ql=ql, kvl=kvl, heads=heads, rope=rope, tm=tm_lat, stack_ckv=slot(0), stack_kr=slot(1))

    if past is None:
        s_len = t_len
        c_all, kr_all, sbk_all, sbv_all = ckvb, kropeb, skb, svb
    else:
        p_ckv, p_kr, p_sbk, p_sbv = past
        past_len = p_ckv.shape[1]
        s_len = -(-(past_len + t_len) // MXU_DIM) * MXU_DIM
        fill = s_len - past_len - t_len

        def rows(cached, new):
            width = new.shape[-1]
            return jnp.concatenate([cached.astype(BF16), new.reshape(batch, t_len, width),
                                    jnp.zeros((batch, fill, width), BF16)], axis=1).reshape(batch * s_len, width)

        def head_rows(cached, new):
            cached = cached.astype(BF16).transpose(2, 0, 1, 3)
            new = new.reshape(sbh, batch, t_len, LANES)
            zeros = jnp.zeros((sbh, batch, fill, LANES), BF16)
            return jnp.concatenate([cached, new, zeros], axis=2).reshape(sbh, batch * s_len, LANES)

        c_all = rows(p_ckv, ckvb)
        kr_all = rows(jnp.pad(p_kr, ((0, 0), (0, 0), (0, LANES - rope))), kropeb)
        sbk_all = head_rows(p_sbk, skb)
        sbv_all = head_rows(p_sbv, svb)

    k_mla, v_mla = kv_expand(c_all, lw["wk"], lw["wv"], gains["g_k_nope"].reshape(1, -1), heads=heads)
    common = dict(batch=batch, t_len=t_len, s_len=s_len, q_off=q_off)
    mla_out = mla_attention(q_mla, k_mla, kr_all, v_mla, heads=heads, hg=min(4, heads), tq=tq_mla, tk=tk_mla,
                            rc=64, **common)
    sb_out = sb_attention(sq, sbk_all, sbv_all, heads=sbh, hg=min(4, sbh), tq=tq_sb, tk=tk_sb, rc=128,
                          **common)

    merged = norm_cast([mla_out, sb_out], [gains["g_out_mla"], gains["g_out_sb"]])
    (h,) = matmul(merged, lw["w_o"], [(F32, "tok")], res=x, tn=512)
    hn = norm_cast([h], [gains["g_mlp"]])
    (u,) = matmul(hn, lw["w_up"], [(BF16, "tok")], act="relu2")
    (y,) = matmul(u, lw["w_down"], [(F32, "tok")], res=h, tk=2048)
    return y, (ckv, krope, sk, sv)


def kernel(x_prompt, x_sample, cache_mla_ckv, cache_mla_krope, cache_sb_k, cache_sb_v,
           g_attn, w_in, g_q_a, w_q_b, g_kv_a, w_kv_b, g_q_nope, g_q_rope, g_k_nope, g_k_rope,
           g_out_mla, g_out_sb, w_o, g_mlp, w_up, w_down):
    depth = w_in.shape[0]
    bp, tp, d = x_prompt.shape
    bs, ts, _ = x_sample.shape
    past_len = cache_mla_ckv.shape[2]
    sbh, sbd = cache_sb_k.shape[-2:]
    nope, rope = g_q_nope.shape[-1], g_q_rope.shape[-1]
    ql, kvl = g_q_a.shape[-1], g_kv_a.shape[-1]
    heads = w_q_b.shape[-1] // (nope + rope)
    vdim = w_kv_b.shape[-1] // heads - nope
    assert nope == LANES and vdim == LANES and sbd == LANES and 2 * rope == LANES
    dims = dict(ql=ql, kvl=kvl, rope=rope, nope=nope, vdim=vdim, heads=heads, sbh=sbh, sbw=sbh * sbd)

    tab_p = _rope_table(jnp.arange(tp, dtype=jnp.int32), rope)
    tab_s = _rope_table(past_len + jnp.arange(ts, dtype=jnp.int32), rope)

    hp = x_prompt.reshape(bp * tp, d)
    hs = x_sample.reshape(bs * ts, d)
    rows_p = rows_s = None
    for l in range(depth):
        lw = _prep_layer(l, dims, w_in, w_q_b, w_kv_b, w_o, w_up, w_down, g_q_nope, g_q_rope, g_k_rope)
        gains = dict(g_attn=g_attn[l], g_q_a=g_q_a[l], g_kv_a=g_kv_a[l], g_k_nope=g_k_nope[l],
                     g_out_mla=g_out_mla[l], g_out_sb=g_out_sb[l], g_mlp=g_mlp[l])
        hp, rows_p = _layer(hp, None, tab_p, lw, gains, dims, rows_p, l, depth, batch=bp, t_len=tp, q_off=0,
                            tq_mla=512, tk_mla=512, tq_sb=256, tk_sb=256, tm_lat=256)
        past = (cache_mla_ckv[l], cache_mla_krope[l], cache_sb_k[l], cache_sb_v[l])
        hs, rows_s = _layer(hs, past, tab_s, lw, gains, dims, rows_s, l, depth, batch=bs, t_len=ts,
                            q_off=past_len, tq_mla=ts, tk_mla=MXU_DIM, tq_sb=ts, tk_sb=MXU_DIM, tm_lat=ts)

    def leaves(rows, b, t):
        shapes = ((kvl,), (rope,), (sbh, sbd), (sbh, sbd))
        return tuple(r.reshape(depth, b, t, *s) for r, s in zip(rows, shapes))

    return (hp.reshape(bp, tp, d), hs.reshape(bs, ts, d)) + leaves(rows_p, bp, tp) + leaves(rows_s, bs, ts)
```

```python
import functools

import jax
import jax.numpy as jnp
from jax import lax
from jax.experimental import pallas as pl
from jax.experimental.pallas import tpu as pltpu

EPS = 1e-6
NEG_INF = -1e30
CHUNK = 64
ROPE_THETA = 10000.0
LANES = 128
MXU_DIM = 256
VMEM_LIMIT = 60 * 1024 * 1024

F32 = jnp.float32
BF16 = jnp.bfloat16


def _pick(n, cap, mult=LANES):
    if n <= cap:
        return n
    best = None
    for d in range(mult, cap + 1, mult):
        if n % d == 0:
            best = d
    assert best is not None, (n, cap, mult)
    return best


def _params(*sem):
    return pltpu.CompilerParams(dimension_semantics=sem, vmem_limit_bytes=VMEM_LIMIT)


def _rms(x, g):
    return x * lax.rsqrt(jnp.mean(x * x, axis=-1, keepdims=True) + EPS) * g


def _norm_cast_kernel(*refs, n_in):
    o_ref = refs[2 * n_in]
    off = 0
    for x_ref, g_ref in zip(refs[:n_in], refs[n_in:2 * n_in]):
        w = x_ref.shape[1]
        o_ref[:, off:off + w] = _rms(x_ref[...], g_ref[...]).astype(o_ref.dtype)
        off += w


def norm_cast(xs, gs, tm=256):
    m = xs[0].shape[0]
    tm = _pick(m, tm, 8)
    widths = [x.shape[1] for x in xs]
    n = len(xs)
    return pl.pallas_call(
        functools.partial(_norm_cast_kernel, n_in=n),
        grid=(m // tm,),
        in_specs=[pl.BlockSpec((tm, w), lambda i: (i, 0)) for w in widths]
        + [pl.BlockSpec((1, w), lambda i: (0, 0)) for w in widths],
        out_specs=pl.BlockSpec((tm, sum(widths)), lambda i: (i, 0)),
        out_shape=jax.ShapeDtypeStruct((m, sum(widths)), BF16),
        compiler_params=_params("parallel"),
        name="norm_cast",
    )(*xs, *[g.reshape(1, -1) for g in gs])


def _cast_kernel(x_ref, o_ref):
    o_ref[...] = x_ref[...].astype(o_ref.dtype)


def cast_layer(w, layer, tr=512, tc=4096):
    _, r, c = w.shape
    tr = _pick(r, tr, 16)
    tc = _pick(c, tc)
    return pl.pallas_call(
        _cast_kernel,
        grid=(r // tr, c // tc),
        in_specs=[pl.BlockSpec((None, tr, tc), lambda i, j: (layer, i, j))],
        out_specs=pl.BlockSpec((tr, tc), lambda i, j: (i, j)),
        out_shape=jax.ShapeDtypeStruct((r, c), BF16),
        compiler_params=_params("parallel", "parallel"),
        name="cast_layer",
    )(w)


def _mm_kernel(*refs, nk, act, has_res, n_aliased, layouts, acc_in_out):
    x_ref, w_ref = refs[0], refs[1]
    p = 2
    res_ref = None
    if has_res:
        res_ref = refs[p]
        p += 1
    p += n_aliased
    out_refs = refs[p:p + len(layouts)]

    def finish(r):
        if act == "relu2":
            r = jnp.maximum(r, 0.0)
            r = r * r
        if has_res:
            r = r + res_ref[...]
        for o_ref, layout in zip(out_refs, layouts):
            if layout == "tok":
                o_ref[...] = r.astype(o_ref.dtype)
            else:
                for hh in range(o_ref.shape[0]):
                    o_ref[hh] = r[:, hh * LANES:(hh + 1) * LANES].astype(o_ref.dtype)

    part = jnp.dot(x_ref[...], w_ref[...], preferred_element_type=F32)
    if nk == 1:
        finish(part)
    elif acc_in_out:
        o_ref = out_refs[0]
        k = pl.program_id(2)

        @pl.when(k == 0)
        def _():
            o_ref[...] = part + res_ref[...] if has_res else part

        @pl.when(k > 0)
        def _():
            o_ref[...] += part
    else:
        acc_ref = refs[p + len(layouts)]
        k = pl.program_id(2)

        @pl.when(k == 0)
        def _():
            acc_ref[...] = part

        @pl.when(jnp.logical_and(k > 0, k < nk - 1))
        def _():
            acc_ref[...] += part

        @pl.when(k == nk - 1)
        def _():
            finish(acc_ref[...] + part)


class Stack:
    def __init__(self, prev, layer, depth):
        self.prev, self.layer, self.depth = prev, layer, depth


def matmul(x, w, outs, act=None, res=None, tm=1024, tn=1024, tk=None, stack=None):
    m, kdim = x.shape
    n = w.shape[1]
    tm = _pick(m, tm, 8)
    tn = _pick(n, tn)
    tk = kdim if tk is None else _pick(kdim, tk)
    nk = kdim // tk
    grid = (m // tm, n // tn, nk)
    in_specs = [pl.BlockSpec((tm, tk), lambda i, j, k: (i, k)),
                pl.BlockSpec((tk, tn), lambda i, j, k: (k, j))]
    args = [x, w]
    if res is not None:
        in_specs.append(pl.BlockSpec((tm, tn), lambda i, j, k: (i, j)))
        args.append(res)
    aliases = {}
    if stack is not None and stack.prev is not None:
        aliases[len(args)] = 0
        in_specs.append(pl.BlockSpec(memory_space=pl.ANY))
        args.append(stack.prev)
    out_specs, out_shapes = [], []
    for o, (dtype, layout) in enumerate(outs):
        if layout == "tok" and o == 0 and stack is not None:
            layer = stack.layer
            out_specs.append(pl.BlockSpec((None, tm, tn), lambda i, j, k: (layer, i, j)))
            out_shapes.append(jax.ShapeDtypeStruct((stack.depth, m, n), dtype))
        elif layout == "tok":
            out_specs.append(pl.BlockSpec((tm, tn), lambda i, j, k: (i, j)))
            out_shapes.append(jax.ShapeDtypeStruct((m, n), dtype))
        else:
            out_specs.append(pl.BlockSpec((tn // LANES, tm, LANES), lambda i, j, k: (j, i, 0)))
            out_shapes.append(jax.ShapeDtypeStruct((n // LANES, m, LANES), dtype))
    acc_in_out = nk > 1 and act is None and stack is None and list(outs) == [(F32, "tok")]
    scratch = [pltpu.VMEM((tm, tn), F32)] if nk > 1 and not acc_in_out else []
    return pl.pallas_call(
        functools.partial(_mm_kernel, nk=nk, act=act, has_res=res is not None, n_aliased=len(aliases),
                          acc_in_out=acc_in_out,
                          layouts=tuple(l for _, l in outs)),
        grid=grid,
        in_specs=in_specs,
        out_specs=out_specs,
        out_shape=out_shapes,
        scratch_shapes=scratch,
        input_output_aliases=aliases,
        compiler_params=_params("parallel", "parallel", "arbitrary"),
        name="matmul",
    )(*args)


def _latent_kernel(lat_ref, gqa_ref, wq_ref, qrow_ref, gkva_ref, krow_ref, tab_ref, *refs,
                   ql, kvl, heads, rope):
    q_ref, ckv_ref, ckvb_ref, kr_ref, krb_ref = refs[-5:]
    lat = lat_ref[...]
    tab = tab_ref[...]

    def rotate(x, row):
        t = _rms(x, row) * tab
        return t + pltpu.roll(t, rope, axis=1)

    qn = _rms(lat[:, :ql], gqa_ref[...]).astype(BF16)
    q = jnp.dot(qn, wq_ref[...], preferred_element_type=F32)
    qrow = qrow_ref[...]
    for h in range(heads):
        base = h * 2 * LANES
        nope = _rms(q[:, base:base + LANES], qrow[:, :LANES])
        q_ref[h, :, :LANES] = nope.astype(BF16)
        q_ref[h, :, LANES:] = rotate(q[:, base + LANES:base + 2 * LANES], qrow[:, LANES:]).astype(BF16)

    ckv = _rms(lat[:, ql:ql + kvl], gkva_ref[...])
    ckv_ref[...] = ckv
    ckvb_ref[...] = ckv.astype(BF16)

    kr = rotate(lat[:, ql + kvl:ql + kvl + LANES], krow_ref[...])
    kr_ref[...] = kr[:, :rope]
    lane = lax.broadcasted_iota(jnp.int32, kr.shape, 1)
    krb_ref[...] = jnp.where(lane < rope, kr, 0.0).astype(BF16)


def latent_post(lat, gqa, wq, qrow, gkva, krow, tab, *, ql, kvl, heads, rope, tm, stack_ckv, stack_kr):
    m = lat.shape[0]
    t = tab.shape[0]
    tm = min(tm, t)
    assert t % tm == 0 and m % tm == 0
    nt = t // tm
    layer, depth = stack_ckv.layer, stack_ckv.depth
    full = lambda a: pl.BlockSpec(a.shape, lambda i: (0,) * a.ndim)
    args = [lat, gqa, wq, qrow, gkva, krow, tab]
    in_specs = [pl.BlockSpec((tm, lat.shape[1]), lambda i: (i, 0)),
                full(gqa), full(wq), full(qrow), full(gkva), full(krow),
                pl.BlockSpec((tm, LANES), lambda i: (i % nt, 0))]
    aliases = {}
    for prev, out_idx in ((stack_ckv.prev, 1), (stack_kr.prev, 3)):
        if prev is not None:
            aliases[len(args)] = out_idx
            in_specs.append(pl.BlockSpec(memory_space=pl.ANY))
            args.append(prev)
    return pl.pallas_call(
        functools.partial(_latent_kernel, ql=ql, kvl=kvl, heads=heads, rope=rope),
        grid=(m // tm,),
        in_specs=in_specs,
        out_specs=[pl.BlockSpec((heads, tm, 2 * LANES), lambda i: (0, i, 0)),
                   pl.BlockSpec((None, tm, kvl), lambda i: (layer, i, 0)),
                   pl.BlockSpec((tm, kvl), lambda i: (i, 0)),
                   pl.BlockSpec((None, tm, rope), lambda i: (layer, i, 0)),
                   pl.BlockSpec((tm, LANES), lambda i: (i, 0))],
        out_shape=[jax.ShapeDtypeStruct((heads, m, 2 * LANES), BF16),
                   jax.ShapeDtypeStruct((depth, m, kvl), F32),
                   jax.ShapeDtypeStruct((m, kvl), BF16),
                   jax.ShapeDtypeStruct((depth, m, rope), F32),
                   jax.ShapeDtypeStruct((m, LANES), BF16)],
        input_output_aliases=aliases,
        compiler_params=_params("parallel"),
        name="latent_post",
    )(*args)


def _kv_expand_kernel(c_ref, wk_ref, wv_ref, g_ref, k_ref, v_ref, *, heads):
    c = c_ref[...]
    k = jnp.dot(c, wk_ref[...], preferred_element_type=F32)
    v = jnp.dot(c, wv_ref[...], preferred_element_type=F32)
    g = g_ref[...]
    for h in range(heads):
        sl = slice(h * LANES, (h + 1) * LANES)
        k_ref[h] = _rms(k[:, sl], g).astype(BF16)
        v_ref[h] = v[:, sl].astype(BF16)


def kv_expand(c, wk, wv, g, *, heads, tm=512):
    rows = c.shape[0]
    tm = _pick(rows, tm, 16)
    full = lambda a: pl.BlockSpec(a.shape, lambda i: (0,) * a.ndim)
    hm = jax.ShapeDtypeStruct((heads, rows, LANES), BF16)
    return pl.pallas_call(
        functools.partial(_kv_expand_kernel, heads=heads),
        grid=(rows // tm,),
        in_specs=[pl.BlockSpec((tm, c.shape[1]), lambda i: (i, 0)), full(wk), full(wv), full(g)],
        out_specs=[pl.BlockSpec((heads, tm, LANES), lambda i: (0, i, 0))] * 2,
        out_shape=[hm, hm],
        compiler_params=_params("parallel"),
        name="kv_expand",
    )(c, wk, wv, g)


def _lane_tile(x, reps):
    return x if reps == 1 else jnp.concatenate([x] * reps, axis=1)


LOG2E = 1.4426950408889634
SB_LOG2_ZERO = -160.0


def _side_effect_loop(lo, hi, fn):
    lax.fori_loop(lo, hi, lambda j, c: (fn(j), c)[1], 0)


def _mla_kernel(q_ref, k_ref, kr_ref, v_ref, o_ref, s_ref, p_ref, m_ref, l_ref, a_ref, acc_ref,
                *, hg, tq, tk, rc, q_off, t_len, s_len):
    nkb = s_len // tk
    heads = range(hg)

    def q_block(qi):
        rows = pl.ds(pl.multiple_of(qi * tq, tq), tq)
        q0 = q_off + pl.program_id(2) * t_len + qi * tq
        n_full = jnp.minimum(((q0 // CHUNK + 1) * CHUNK) // tk, nkb)
        n_kv = jnp.minimum((((q0 + tq - 1) // CHUNK + 1) * CHUNK + tk - 1) // tk, nkb)
        m_ref[...] = jnp.full(m_ref.shape, NEG_INF, F32)
        l_ref[...] = jnp.zeros(l_ref.shape, F32)
        acc_ref[...] = jnp.zeros(acc_ref.shape, F32)

        def step(j, masked):
            ks = pl.ds(pl.multiple_of(j * tk, tk), tk)
            kr = kr_ref[ks, :]
            for g in heads:
                k = jnp.concatenate([k_ref[g, ks, :], kr], axis=1)
                s_ref[g] = lax.dot_general(q_ref[g, rows, :], k, (((1,), (1,)), ((), ())),
                                           preferred_element_type=F32)
            for g in heads:
                for c in range(tq // rc):
                    rs = slice(c * rc, (c + 1) * rc)
                    s = s_ref[g, rs, :]
                    if masked:
                        qc = (q0 + c * rc + lax.broadcasted_iota(jnp.int32, (rc, 1), 0)) // CHUNK
                        kc = (j * tk + lax.broadcasted_iota(jnp.int32, (1, tk), 1)) // CHUNK
                        s = jnp.where(kc <= qc, s, NEG_INF)
                    m_prev = m_ref[g, rs, :]
                    m_next = jnp.maximum(m_prev, jnp.max(s, axis=1, keepdims=True))
                    alpha = jnp.exp2(m_prev - m_next)
                    p = jnp.exp2(s - _lane_tile(m_next, tk // LANES))
                    l_ref[g, rs, :] = alpha * l_ref[g, rs, :] + jnp.sum(p, axis=1, keepdims=True)
                    m_ref[g, rs, :] = m_next
                    a_ref[g, rs, :] = alpha
                    p_ref[g, rs, :] = p.astype(BF16)
            for g in heads:
                acc_ref[g] = a_ref[g] * acc_ref[g] + jnp.dot(p_ref[g], v_ref[g, ks, :],
                                                             preferred_element_type=F32)

        _side_effect_loop(0, n_full, lambda j: step(j, False))
        _side_effect_loop(n_full, n_kv, lambda j: step(j, True))
        for g in heads:
            o_ref[rows, g * LANES:(g + 1) * LANES] = acc_ref[g] / l_ref[g]

    _side_effect_loop(0, t_len // tq, q_block)


def _sb_kernel(q_ref, k_ref, v_ref, u_ref, o_ref, z_ref, later_ref, hi_ref, lo_ref, r_ref, rsum_ref,
               acc_ref, *, hg, tq, tk, rc, q_off, t_len, s_len):
    nkb = s_len // tk
    reps = tk // LANES
    heads = range(hg)

    def q_block(qi):
        rows = pl.ds(pl.multiple_of(qi * tq, tq), tq)
        q0 = q_off + pl.program_id(2) * t_len + qi * tq
        n_full = jnp.minimum(q0 // tk, nkb)
        n_kv = jnp.minimum((q0 + tq - 1 + tk - 1) // tk, nkb)
        r_ref[...] = jnp.zeros(r_ref.shape, F32)
        acc_ref[...] = jnp.zeros(acc_ref.shape, F32)

        def step(j, masked):
            ks = pl.ds(pl.multiple_of(j * tk, tk), tk)

            def before(c):
                qp = q0 + c * rc + lax.broadcasted_iota(jnp.int32, (rc, 1), 0)
                kp = j * tk + lax.broadcasted_iota(jnp.int32, (1, tk), 1)
                return kp < qp

            for g in heads:
                z_ref[g] = lax.dot_general(q_ref[g, rows, :], k_ref[g, ks, :], (((1,), (1,)), ((), ())),
                                           preferred_element_type=F32)
            for g in heads:
                for c in range(tq // rc):
                    rs = slice(c * rc, (c + 1) * rc)
                    z = z_ref[g, rs, :]
                    fail = jnp.maximum(z, 0.0) + jnp.log(1.0 + jnp.exp2(-jnp.abs(z))) * LOG2E
                    z_ref[g, rs, :] = z - fail
                    if masked:
                        fail = jnp.where(before(c), fail, 0.0)
                    hi = fail.astype(BF16)
                    hi_ref[g, rs, :] = hi
                    lo_ref[g, rs, :] = (fail - hi.astype(F32)).astype(BF16)
                    rsum_ref[g, rs, :] = jnp.broadcast_to(jnp.sum(fail, axis=1, keepdims=True), (rc, LANES))
            u = u_ref[...]
            for g in heads:
                later_ref[g] = (jnp.dot(hi_ref[g], u, preferred_element_type=F32)
                                + jnp.dot(lo_ref[g], u, preferred_element_type=F32))
            for g in heads:
                for c in range(tq // rc):
                    rs = slice(c * rc, (c + 1) * rc)
                    r_prev = r_ref[g, rs, :]
                    w = jnp.exp2(z_ref[g, rs, :] - later_ref[g, rs, :] - _lane_tile(r_prev, reps))
                    if masked:
                        w = jnp.where(before(c), w, 0.0)
                    hi_ref[g, rs, :] = w.astype(BF16)
                    r_ref[g, rs, :] = r_prev + rsum_ref[g, rs, :]
            for g in heads:
                acc_ref[g] += jnp.dot(hi_ref[g], v_ref[g, ks, :], preferred_element_type=F32)

        _side_effect_loop(0, n_kv - n_full, lambda t: step(n_kv - 1 - t, True))

        def more(c):
            return jnp.logical_and(c[0] >= 0, c[1] < -SB_LOG2_ZERO)

        def visit(c):
            step(c[0], False)
            return c[0] - 1, jnp.min(r_ref[...])

        lax.while_loop(more, visit, (n_full - 1, jnp.min(r_ref[...])))
        for g in heads:
            o_ref[rows, g * LANES:(g + 1) * LANES] = acc_ref[g]

    _side_effect_loop(0, t_len // tq, q_block)


def _attention(kernel, q, kv_args, extra, *, batch, heads, hg, t_len, s_len, q_off, tq, tk, rc, scratch, name,
               span=1024):
    tq = min(tq, t_len)
    rc = min(rc, tq)
    span = min(span, t_len)
    nspan = t_len // span
    assert t_len % span == 0 and span % tq == 0 and tq % rc == 0 and s_len % tk == 0 and heads % hg == 0
    in_specs = [pl.BlockSpec((hg, span, q.shape[-1]), lambda b, h, t: (h, b * nspan + t, 0))]
    args = [q]
    for a, per_head in kv_args:
        if per_head:
            in_specs.append(pl.BlockSpec((hg, s_len, a.shape[-1]), lambda b, h, t: (h, b, 0)))
        else:
            in_specs.append(pl.BlockSpec((s_len, a.shape[-1]), lambda b, h, t: (b, 0)))
        args.append(a)
    for a in extra:
        in_specs.append(pl.BlockSpec(a.shape, lambda b, h, t: (0,) * a.ndim))
        args.append(a)
    return pl.pallas_call(
        functools.partial(kernel, hg=hg, tq=tq, tk=tk, rc=rc, q_off=q_off, t_len=span, s_len=s_len),
        grid=(batch, heads // hg, nspan),
        in_specs=in_specs,
        out_specs=pl.BlockSpec((span, hg * LANES), lambda b, h, t: (b * nspan + t, h)),
        out_shape=jax.ShapeDtypeStruct((batch * t_len, heads * LANES), F32),
        scratch_shapes=scratch(hg, tq, tk),
        compiler_params=_params("parallel", "parallel", "parallel"),
        name=name,
    )(*args)


def mla_attention(q, k, kr, v, **kw):
    scratch = lambda hg, tq, tk: ([pltpu.VMEM((hg, tq, tk), F32), pltpu.VMEM((hg, tq, tk), BF16)]
                                  + [pltpu.VMEM((hg, tq, LANES), F32)] * 4)
    return _attention(_mla_kernel, q, [(k, True), (kr, False), (v, True)], [], scratch=scratch,
                      name="mla_attention", **kw)


def sb_attention(q, k, v, **kw):
    tk = kw["tk"]
    u = (lax.broadcasted_iota(jnp.int32, (tk, tk), 0)
         > lax.broadcasted_iota(jnp.int32, (tk, tk), 1)).astype(BF16)
    scratch = lambda hg, tq, tk: ([pltpu.VMEM((hg, tq, tk), F32)] * 2 + [pltpu.VMEM((hg, tq, tk), BF16)] * 2
                                  + [pltpu.VMEM((hg, tq, LANES), F32)] * 3)
    return _attention(_sb_kernel, q, [(k, True), (v, True)], [u], scratch=scratch,
                      name="sb_attention", **kw)


def _rope_table(pos, rope):
    half = rope // 2
    inv_freq = jnp.power(ROPE_THETA, -jnp.arange(half, dtype=F32) / half)
    ang = pos.astype(F32)[:, None] * inv_freq[None, :]
    cos, sin = jnp.cos(ang), jnp.sin(ang)
    return jnp.concatenate([cos, cos, -sin, sin], axis=1)


def _swap_halves(a, axis=-1):
    lo, hi = jnp.split(a, 2, axis=axis)
    return jnp.concatenate([hi, lo], axis=axis)


def _prep_layer(l, dims, w_in, w_q_b, w_kv_b, w_o, w_up, w_down, g_q_nope, g_q_rope, g_k_rope):
    ql, kvl, rope, nope, vdim, heads, sbw = (dims[k] for k in
                                             ("ql", "kvl", "rope", "nope", "vdim", "heads", "sbw"))
    wi = w_in[l]
    d = wi.shape[0]
    kr = wi[:, ql + kvl:ql + kvl + rope]
    lat_w = ql + kvl + 2 * rope
    pad = (-lat_w) % MXU_DIM
    w_lat = jnp.concatenate([wi[:, :ql + kvl], kr, _swap_halves(kr), jnp.zeros((d, pad), F32)], axis=1)
    o = ql + kvl + rope
    w_sq, w_sk, w_sv = (wi[:, o + n * sbw:o + (n + 1) * sbw] for n in range(3))

    wq = w_q_b[l].reshape(ql, heads, nope + rope)
    wq_r = wq[:, :, nope:]
    wq = jnp.concatenate([wq[:, :, :nope], wq_r, _swap_halves(wq_r)], axis=2).reshape(ql, heads * 2 * LANES)

    wkv = w_kv_b[l].reshape(kvl, heads, nope + vdim)
    wk = wkv[:, :, :nope].reshape(kvl, heads * nope)
    wv = wkv[:, :, nope:].reshape(kvl, heads * vdim)

    scale = LOG2E * (nope + rope) ** -0.5
    qrow = (jnp.concatenate([g_q_nope[l], g_q_rope[l], _swap_halves(g_q_rope[l])]) * scale).reshape(1, -1)
    krow = jnp.concatenate([g_k_rope[l], _swap_halves(g_k_rope[l])]).reshape(1, -1)
    bf = lambda a: a.astype(BF16)
    sb_scale = LOG2E * (sbw // dims["sbh"]) ** -0.5
    return dict(w_lat=bf(w_lat), w_sq=bf(w_sq * sb_scale), w_sk=bf(w_sk), w_sv=bf(w_sv),
                wq=bf(wq), wk=bf(wk), wv=bf(wv),
                w_o=cast_layer(w_o, l), w_up=cast_layer(w_up, l), w_down=cast_layer(w_down, l),
                qrow=qrow, krow=krow)


def _layer(x, past, tab, lw, gains, dims, leaves, layer, depth, *, batch, t_len, q_off, tq_mla, tk_mla,
           tq_sb, tk_sb, tm_lat):
    ql, kvl, rope, heads, sbh = (dims[k] for k in ("ql", "kvl", "rope", "heads", "sbh"))
    prev = leaves if leaves is not None else (None,) * 4
    slot = lambda n: Stack(prev[n], layer, depth)
    xn = norm_cast([x], [gains["g_attn"]])
    (sq,) = matmul(xn, lw["w_sq"], [(BF16, "head")])
    sk, skb = matmul(xn, lw["w_sk"], [(F32, "tok"), (BF16, "head")], stack=slot(2))
    sv, svb = matmul(xn, lw["w_sv"], [(F32, "tok"), (BF16, "head")], stack=slot(3))
    (lat,) = matmul(xn, lw["w_lat"], [(F32, "tok")])
    q_mla, ckv, ckvb, krope, kropeb = latent_post(
        lat, gains["g_q_a"].reshape(1, -1), lw["wq"], lw["qrow"], gains["g_kv_a"].reshape(1, -1),
        lw["krow"], tab, ql=ql, kvl=kvl, heads=heads, rope=rope, tm=tm_lat, stack_ckv=slot(0), stack_kr=slot(1))

    if past is None:
        s_len = t_len
        c_all, kr_all, sbk_all, sbv_all = ckvb, kropeb, skb, svb
    else:
        p_ckv, p_kr, p_sbk, p_sbv = past
        past_len = p_ckv.shape[1]
        s_len = -(-(past_len + t_len) // MXU_DIM) * MXU_DIM
        fill = s_len - past_len - t_len

        def rows(cached, new):
            width = new.shape[-1]
            return jnp.concatenate([cached.astype(BF16), new.reshape(batch, t_len, width),
                                    jnp.zeros((batch, fill, width), BF16)], axis=1).reshape(batch * s_len, width)

        def head_rows(cached, new):
            cached = cached.astype(BF16).transpose(2, 0, 1, 3)
            new = new.reshape(sbh, batch, t_len, LANES)
            zeros = jnp.zeros((sbh, batch, fill, LANES), BF16)
            return jnp.concatenate([cached, new, zeros], axis=2).reshape(sbh, batch * s_len, LANES)

        c_all = rows(p_ckv, ckvb)
        kr_all = rows(jnp.pad(p_kr, ((0, 0), (0, 0), (0, LANES - rope))), kropeb)
        sbk_all = head_rows(p_sbk, skb)
        sbv_all = head_rows(p_sbv, svb)

    k_mla, v_mla = kv_expand(c_all, lw["wk"], lw["wv"], gains["g_k_nope"].reshape(1, -1), heads=heads)
    common = dict(batch=batch, t_len=t_len, s_len=s_len, q_off=q_off)
    mla_out = mla_attention(q_mla, k_mla, kr_all, v_mla, heads=heads, hg=min(4, heads), tq=tq_mla, tk=tk_mla,
                            rc=64, **common)
    sb_out = sb_attention(sq, sbk_all, sbv_all, heads=sbh, hg=min(4, sbh), tq=tq_sb, tk=tk_sb, rc=128,
                          **common)

    merged = norm_cast([mla_out, sb_out], [gains["g_out_mla"], gains["g_out_sb"]])
    (h,) = matmul(merged, lw["w_o"], [(F32, "tok")], res=x)
    hn = norm_cast([h], [gains["g_mlp"]])
    (u,) = matmul(hn, lw["w_up"], [(BF16, "tok")], act="relu2")
    (y,) = matmul(u, lw["w_down"], [(F32, "tok")], res=h, tk=4096)
    return y, (ckv, krope, sk, sv)


def kernel(x_prompt, x_sample, cache_mla_ckv, cache_mla_krope, cache_sb_k, cache_sb_v,
           g_attn, w_in, g_q_a, w_q_b, g_kv_a, w_kv_b, g_q_nope, g_q_rope, g_k_nope, g_k_rope,
           g_out_mla, g_out_sb, w_o, g_mlp, w_up, w_down):
    depth = w_in.shape[0]
    bp, tp, d = x_prompt.shape
    bs, ts, _ = x_sample.shape
    past_len = cache_mla_ckv.shape[2]
    sbh, sbd = cache_sb_k.shape[-2:]
    nope, rope = g_q_nope.shape[-1], g_q_rope.shape[-1]
    ql, kvl = g_q_a.shape[-1], g_kv_a.shape[-1]
    heads = w_q_b.shape[-1] // (nope + rope)
    vdim = w_kv_b.shape[-1] // heads - nope
    assert nope == LANES and vdim == LANES and sbd == LANES and 2 * rope == LANES
    dims = dict(ql=ql, kvl=kvl, rope=rope, nope=nope, vdim=vdim, heads=heads, sbh=sbh, sbw=sbh * sbd)

    tab_p = _rope_table(jnp.arange(tp, dtype=jnp.int32), rope)
    tab_s = _rope_table(past_len + jnp.arange(ts, dtype=jnp.int32), rope)

    hp = x_prompt.reshape(bp * tp, d)
    hs = x_sample.reshape(bs * ts, d)
    rows_p = rows_s = None
    for l in range(depth):
        lw = _prep_layer(l, dims, w_in, w_q_b, w_kv_b, w_o, w_up, w_down, g_q_nope, g_q_rope, g_k_rope)
        gains = dict(g_attn=g_attn[l], g_q_a=g_q_a[l], g_kv_a=g_kv_a[l], g_k_nope=g_k_nope[l],
                     g_out_mla=g_out_mla[l], g_out_sb=g_out_sb[l], g_mlp=g_mlp[l])
        hp, rows_p = _layer(hp, None, tab_p, lw, gains, dims, rows_p, l, depth, batch=bp, t_len=tp, q_off=0,
                            tq_mla=512, tk_mla=512, tq_sb=256, tk_sb=256, tm_lat=256)
        past = (cache_mla_ckv[l], cache_mla_krope[l], cache_sb_k[l], cache_sb_v[l])
        hs, rows_s = _layer(hs, past, tab_s, lw, gains, dims, rows_s, l, depth, batch=bs, t_len=ts,
                            q_off=past_len, tq_mla=ts, tk_mla=MXU_DIM, tq_sb=ts, tk_sb=MXU_DIM, tm_lat=ts)

    def leaves(rows, b, t):
        shapes = ((kvl,), (rope,), (sbh, sbd), (sbh, sbd))
        return tuple(r.reshape(depth, b, t, *s) for r, s in zip(rows, shapes))

    return (hp.reshape(bp, tp, d), hs.reshape(bs, ts, d)) + leaves(rows_p, bp, tp) + leaves(rows_s, bs, ts)
```

```python
import functools

import jax
import jax.numpy as jnp
from jax import lax
from jax.experimental import pallas as pl
from jax.experimental.pallas import tpu as pltpu

EPS = 1e-6
NEG_INF = -1e30
CHUNK = 64
ROPE_THETA = 10000.0
LANES = 128
MXU_DIM = 256
VMEM_LIMIT = 60 * 1024 * 1024
LOG2E = 1.4426950408889634
SB_LOG2_ZERO = -160.0

F32 = jnp.float32
BF16 = jnp.bfloat16


def _pick(n, cap, mult=LANES):
    if n <= cap:
        return n
    best = None
    for d in range(mult, cap + 1, mult):
        if n % d == 0:
            best = d
    assert best is not None, (n, cap, mult)
    return best


def _params(*sem):
    return pltpu.CompilerParams(dimension_semantics=sem, vmem_limit_bytes=VMEM_LIMIT)


def _rms(x, g):
    return x * lax.rsqrt(jnp.mean(x * x, axis=-1, keepdims=True) + EPS) * g


def _lane_tile(x, reps):
    return x if reps == 1 else jnp.concatenate([x] * reps, axis=1)


def _row_ssq(x):
    return jnp.broadcast_to(jnp.sum(x * x, axis=1, keepdims=True), (x.shape[0], LANES))


def _side_effect_loop(lo, hi, fn):
    lax.fori_loop(lo, hi, lambda j, c: (fn(j), c)[1], 0)


class Stack:
    def __init__(self, prev, layer, depth):
        self.prev, self.layer, self.depth = prev, layer, depth


def _cast_ssq_kernel(x_ref, o_ref, ssq_ref):
    x = x_ref[...]
    o_ref[...] = x.astype(o_ref.dtype)
    ssq_ref[...] = _row_ssq(x)


def cast_ssq(x, tm=256):
    m, d = x.shape
    tm = _pick(m, tm, 16)
    return pl.pallas_call(
        _cast_ssq_kernel,
        grid=(m // tm,),
        in_specs=[pl.BlockSpec((tm, d), lambda i: (i, 0))],
        out_specs=[pl.BlockSpec((tm, d), lambda i: (i, 0)),
                   pl.BlockSpec((None, None, tm, LANES), lambda i: (0, 0, i, 0))],
        out_shape=[jax.ShapeDtypeStruct((m, d), BF16), jax.ShapeDtypeStruct((1, 1, m, LANES), F32)],
        compiler_params=_params("parallel"),
        name="cast_ssq",
    )(x)


def _cast_kernel(x_ref, *refs):
    o_ref = refs[-1]
    x = x_ref[...]
    if len(refs) == 2:
        x = x * refs[0][...]
    o_ref[...] = x.astype(o_ref.dtype)


def cast_layer(w, layer, gain=None, tr=512, tc=4096):
    _, r, c = w.shape
    tr = _pick(r, tr, 16)
    tc = _pick(c, tc)
    in_specs = [pl.BlockSpec((None, tr, tc), lambda i, j: (layer, i, j))]
    args = [w]
    if gain is not None:
        in_specs.append(pl.BlockSpec((tr, 1), lambda i, j: (i, 0)))
        args.append(gain.reshape(r, 1))
    return pl.pallas_call(
        _cast_kernel,
        grid=(r // tr, c // tc),
        in_specs=in_specs,
        out_specs=pl.BlockSpec((tr, tc), lambda i, j: (i, j)),
        out_shape=jax.ShapeDtypeStruct((r, c), BF16),
        compiler_params=_params("parallel", "parallel"),
        name="cast_layer",
    )(*args)


def _mm_kernel(*refs, nk, act, parts, inv_width, has_res, n_aliased, n_out, emit_ssq, acc_in_out):
    x_ref, w_ref = refs[0], refs[1]
    p = 2
    ssq_in_ref = res_ref = ssq_out_ref = None
    if parts:
        ssq_in_ref = refs[p]
        p += 1
    if has_res:
        res_ref = refs[p]
        p += 1
    p += n_aliased
    out_refs = refs[p:p + n_out]
    p += n_out
    if emit_ssq:
        ssq_out_ref = refs[p]
        p += 1

    def finish(r):
        if act == "relu2":
            r = jnp.maximum(r, 0.0)
            r = r * r
        if has_res:
            r = r + res_ref[...]
        for o_ref in out_refs:
            o_ref[...] = r.astype(o_ref.dtype)
        if emit_ssq:
            ssq_out_ref[...] = _row_ssq(r)

    part = jnp.dot(x_ref[...], w_ref[...], preferred_element_type=F32)
    if parts:
        ssq = ssq_in_ref[0]
        for n in range(1, parts):
            ssq = ssq + ssq_in_ref[n]
        part = part * _lane_tile(lax.rsqrt(ssq * inv_width + EPS), part.shape[1] // LANES)
    if nk == 1:
        finish(part)
        return
    acc_ref = out_refs[0] if acc_in_out else refs[p]
    k = pl.program_id(2)

    @pl.when(k == 0)
    def _():
        acc_ref[...] = part

    @pl.when(jnp.logical_and(k > 0, k < nk - 1))
    def _():
        acc_ref[...] += part

    @pl.when(k == nk - 1)
    def _():
        finish(acc_ref[...] + part)


def matmul(x, w, outs, act=None, res=None, ssq=None, emit_ssq=False, stack=None, tm=1024, tn=1024, tk=None):
    m, kdim = x.shape
    n = w.shape[1]
    tm = _pick(m, tm, 8)
    tn = _pick(n, tn)
    tk = kdim if tk is None else _pick(kdim, tk)
    nk = kdim // tk
    grid = (m // tm, n // tn, nk)
    in_specs = [pl.BlockSpec((tm, tk), lambda i, j, k: (i, k)),
                pl.BlockSpec((tk, tn), lambda i, j, k: (k, j))]
    args = [x, w]
    parts = 0
    if ssq is not None:
        nks, parts = ssq.shape[:2]
        assert nks == nk
        in_specs.append(pl.BlockSpec((None, parts, tm, LANES),
                                     (lambda i, j, k: (k, 0, i, 0)) if nks > 1 else (lambda i, j, k: (0, 0, i, 0))))
        args.append(ssq)
    if res is not None:
        in_specs.append(pl.BlockSpec((tm, tn), lambda i, j, k: (i, j)))
        args.append(res)
    aliases = {}
    if stack is not None and stack.prev is not None:
        aliases[len(args)] = 0
        in_specs.append(pl.BlockSpec(memory_space=pl.ANY))
        args.append(stack.prev)
    out_specs, out_shapes = [], []
    for o, dtype in enumerate(outs):
        if o == 0 and stack is not None:
            layer = stack.layer
            out_specs.append(pl.BlockSpec((None, tm, tn), lambda i, j, k: (layer, i, j)))
            out_shapes.append(jax.ShapeDtypeStruct((stack.depth, m, n), dtype))
        else:
            out_specs.append(pl.BlockSpec((tm, tn), lambda i, j, k: (i, j)))
            out_shapes.append(jax.ShapeDtypeStruct((m, n), dtype))
    if emit_ssq:
        out_specs.append(pl.BlockSpec((None, tm, LANES), lambda i, j, k: (j, i, 0)))
        out_shapes.append(jax.ShapeDtypeStruct((n // tn, m, LANES), F32))
    acc_in_out = nk > 1 and act is None and stack is None and outs[0] == F32
    scratch = [pltpu.VMEM((tm, tn), F32)] if nk > 1 and not acc_in_out else []
    return pl.pallas_call(
        functools.partial(_mm_kernel, nk=nk, act=act, parts=parts, inv_width=1.0 / tk, has_res=res is not None,
                          n_aliased=len(aliases), n_out=len(outs), emit_ssq=emit_ssq, acc_in_out=acc_in_out),
        grid=grid,
        in_specs=in_specs,
        out_specs=out_specs,
        out_shape=out_shapes,
        scratch_shapes=scratch,
        input_output_aliases=aliases,
        compiler_params=_params("parallel", "parallel", "arbitrary"),
        name="matmul",
    )(*args)


def _latent_kernel(lat_ref, gqa_ref, wq_ref, qrow_ref, gkva_ref, krow_ref, tab_ref, *refs,
                   ql, kvl, heads, rope):
    q_ref, ckv_ref, ckvb_ref, kr_ref, krb_ref = refs[-5:]
    lat = lat_ref[...]
    tab = tab_ref[...]

    def rotate(x, row):
        t = _rms(x, row) * tab
        return t + pltpu.roll(t, rope, axis=1)

    qn = _rms(lat[:, :ql], gqa_ref[...]).astype(BF16)
    q = jnp.dot(qn, wq_ref[...], preferred_element_type=F32)
    qrow = qrow_ref[...]
    for h in range(heads):
        base = h * 2 * LANES
        nope = _rms(q[:, base:base + LANES], qrow[:, :LANES])
        q_ref[:, base:base + LANES] = nope.astype(BF16)
        q_ref[:, base + LANES:base + 2 * LANES] = rotate(q[:, base + LANES:base + 2 * LANES],
                                                         qrow[:, LANES:]).astype(BF16)

    ckv = _rms(lat[:, ql:ql + kvl], gkva_ref[...])
    ckv_ref[...] = ckv
    ckvb_ref[...] = ckv.astype(BF16)

    kr = rotate(lat[:, ql + kvl:ql + kvl + LANES], krow_ref[...])
    kr_ref[...] = kr[:, :rope]
    lane = lax.broadcasted_iota(jnp.int32, kr.shape, 1)
    krb_ref[...] = jnp.where(lane < rope, kr, 0.0).astype(BF16)


def latent_post(lat, gqa, wq, qrow, gkva, krow, tab, *, ql, kvl, heads, rope, tm, stack_ckv, stack_kr):
    m = lat.shape[0]
    t = tab.shape[0]
    tm = min(tm, t)
    assert t % tm == 0 and m % tm == 0
    nt = t // tm
    layer, depth = stack_ckv.layer, stack_ckv.depth
    full = lambda a: pl.BlockSpec(a.shape, lambda i: (0,) * a.ndim)
    args = [lat, gqa, wq, qrow, gkva, krow, tab]
    in_specs = [pl.BlockSpec((tm, lat.shape[1]), lambda i: (i, 0)),
                full(gqa), full(wq), full(qrow), full(gkva), full(krow),
                pl.BlockSpec((tm, LANES), lambda i: (i % nt, 0))]
    aliases = {}
    for prev, out_idx in ((stack_ckv.prev, 1), (stack_kr.prev, 3)):
        if prev is not None:
            aliases[len(args)] = out_idx
            in_specs.append(pl.BlockSpec(memory_space=pl.ANY))
            args.append(prev)
    return pl.pallas_call(
        functools.partial(_latent_kernel, ql=ql, kvl=kvl, heads=heads, rope=rope),
        grid=(m // tm,),
        in_specs=in_specs,
        out_specs=[pl.BlockSpec((tm, heads * 2 * LANES), lambda i: (i, 0)),
                   pl.BlockSpec((None, tm, kvl), lambda i: (layer, i, 0)),
                   pl.BlockSpec((tm, kvl), lambda i: (i, 0)),
                   pl.BlockSpec((None, tm, rope), lambda i: (layer, i, 0)),
                   pl.BlockSpec((tm, LANES), lambda i: (i, 0))],
        out_shape=[jax.ShapeDtypeStruct((m, heads * 2 * LANES), BF16),
                   jax.ShapeDtypeStruct((depth, m, kvl), F32),
                   jax.ShapeDtypeStruct((m, kvl), BF16),
                   jax.ShapeDtypeStruct((depth, m, rope), F32),
                   jax.ShapeDtypeStruct((m, LANES), BF16)],
        input_output_aliases=aliases,
        compiler_params=_params("parallel"),
        name="latent_post",
    )(*args)


def _kv_expand_kernel(c_ref, wk_ref, wv_ref, g_ref, k_ref, v_ref, *, heads):
    c = c_ref[...]
    k = jnp.dot(c, wk_ref[...], preferred_element_type=F32)
    v = jnp.dot(c, wv_ref[...], preferred_element_type=F32)
    g = g_ref[...]
    for h in range(heads):
        sl = slice(h * LANES, (h + 1) * LANES)
        k_ref[:, sl] = _rms(k[:, sl], g).astype(BF16)
    v_ref[...] = v.astype(BF16)


def kv_expand(c, wk, wv, g, *, heads, tm=512):
    rows = c.shape[0]
    tm = _pick(rows, tm, 16)
    full = lambda a: pl.BlockSpec(a.shape, lambda i: (0,) * a.ndim)
    wide = jax.ShapeDtypeStruct((rows, heads * LANES), BF16)
    return pl.pallas_call(
        functools.partial(_kv_expand_kernel, heads=heads),
        grid=(rows // tm,),
        in_specs=[pl.BlockSpec((tm, c.shape[1]), lambda i: (i, 0)), full(wk), full(wv), full(g)],
        out_specs=[pl.BlockSpec((tm, heads * LANES), lambda i: (i, 0))] * 2,
        out_shape=[wide, wide],
        compiler_params=_params("parallel"),
        name="kv_expand",
    )(c, wk, wv, g)


def _emit_heads(o_ref, ssq_ref, rows, outs):
    ssq = None
    for g, o in enumerate(outs):
        o_ref[rows, g * LANES:(g + 1) * LANES] = o.astype(o_ref.dtype)
        ssq = _row_ssq(o) if ssq is None else ssq + _row_ssq(o)
    ssq_ref[rows, :] = ssq


def _mla_kernel(q_ref, k_ref, kr_ref, v_ref, *refs, hg, tq, tk, rc, q_off, t_len, s_len):
    o_ref, ssq_ref, s_ref, p_ref, m_ref, l_ref, a_ref, acc_ref = refs[-8:]
    nkb = s_len // tk
    heads = range(hg)

    def q_block(qi):
        rows = pl.ds(pl.multiple_of(qi * tq, tq), tq)
        q0 = q_off + pl.program_id(2) * t_len + qi * tq
        n_full = jnp.minimum(((q0 // CHUNK + 1) * CHUNK) // tk, nkb)
        n_kv = jnp.minimum((((q0 + tq - 1) // CHUNK + 1) * CHUNK + tk - 1) // tk, nkb)
        m_ref[...] = jnp.full(m_ref.shape, NEG_INF, F32)
        l_ref[...] = jnp.zeros(l_ref.shape, F32)
        acc_ref[...] = jnp.zeros(acc_ref.shape, F32)

        def step(j, masked):
            ks = pl.ds(pl.multiple_of(j * tk, tk), tk)
            kr = kr_ref[ks, :]
            for g in heads:
                k = jnp.concatenate([k_ref[ks, g * LANES:(g + 1) * LANES], kr], axis=1)
                q = q_ref[rows, g * 2 * LANES:(g + 1) * 2 * LANES]
                s_ref[g] = lax.dot_general(q, k, (((1,), (1,)), ((), ())),
                                           preferred_element_type=F32)
            for g in heads:
                for c in range(tq // rc):
                    rs = slice(c * rc, (c + 1) * rc)
                    s = s_ref[g, rs, :]
                    if masked:
                        qc = (q0 + c * rc + lax.broadcasted_iota(jnp.int32, (rc, 1), 0)) // CHUNK
                        kc = (j * tk + lax.broadcasted_iota(jnp.int32, (1, tk), 1)) // CHUNK
                        s = jnp.where(kc <= qc, s, NEG_INF)
                    m_prev = m_ref[g, rs, :]
                    m_next = jnp.maximum(m_prev, jnp.max(s, axis=1, keepdims=True))
                    alpha = jnp.exp2(m_prev - m_next)
                    p = jnp.exp2(s - _lane_tile(m_next, tk // LANES))
                    l_ref[g, rs, :] = alpha * l_ref[g, rs, :] + jnp.sum(p, axis=1, keepdims=True)
                    m_ref[g, rs, :] = m_next
                    a_ref[g, rs, :] = alpha
                    p_ref[g, rs, :] = p.astype(BF16)
            for g in heads:
                acc_ref[g] = a_ref[g] * acc_ref[g] + jnp.dot(p_ref[g], v_ref[ks, g * LANES:(g + 1) * LANES],
                                                             preferred_element_type=F32)

        _side_effect_loop(0, n_full, lambda j: step(j, False))
        _side_effect_loop(n_full, n_kv, lambda j: step(j, True))
        _emit_heads(o_ref, ssq_ref, rows, [acc_ref[g] / l_ref[g] for g in heads])

    _side_effect_loop(0, t_len // tq, q_block)


def _sb_kernel(q_ref, k_ref, v_ref, u_ref, *refs, hg, tq, tk, rc, q_off, t_len, s_len):
    o_ref, ssq_ref, z_ref, later_ref, hi_ref, lo_ref, r_ref, rsum_ref, acc_ref = refs[-9:]
    nkb = s_len // tk
    reps = tk // LANES
    heads = range(hg)

    def q_block(qi):
        rows = pl.ds(pl.multiple_of(qi * tq, tq), tq)
        q0 = q_off + pl.program_id(2) * t_len + qi * tq
        n_full = jnp.minimum(q0 // tk, nkb)
        n_kv = jnp.minimum((q0 + tq - 1 + tk - 1) // tk, nkb)
        r_ref[...] = jnp.zeros(r_ref.shape, F32)
        acc_ref[...] = jnp.zeros(acc_ref.shape, F32)

        def step(j, masked):
            ks = pl.ds(pl.multiple_of(j * tk, tk), tk)

            def before(c):
                qp = q0 + c * rc + lax.broadcasted_iota(jnp.int32, (rc, 1), 0)
                kp = j * tk + lax.broadcasted_iota(jnp.int32, (1, tk), 1)
                return kp < qp

            for g in heads:
                hs = slice(g * LANES, (g + 1) * LANES)
                z_ref[g] = lax.dot_general(q_ref[rows, hs], k_ref[ks, hs], (((1,), (1,)), ((), ())),
                                           preferred_element_type=F32)
            for g in heads:
                for c in range(tq // rc):
                    rs = slice(c * rc, (c + 1) * rc)
                    z = z_ref[g, rs, :]
                    fail = jnp.maximum(z, 0.0) + jnp.log(1.0 + jnp.exp2(-jnp.abs(z))) * LOG2E
                    z_ref[g, rs, :] = z - fail
                    if masked:
                        fail = jnp.where(before(c), fail, 0.0)
                    hi = fail.astype(BF16)
                    hi_ref[g, rs, :] = hi
                    lo_ref[g, rs, :] = (fail - hi.astype(F32)).astype(BF16)
                    rsum_ref[g, rs, :] = jnp.broadcast_to(jnp.sum(fail, axis=1, keepdims=True), (rc, LANES))
            u = u_ref[...]
            for g in heads:
                later_ref[g] = (jnp.dot(hi_ref[g], u, preferred_element_type=F32)
                                + jnp.dot(lo_ref[g], u, preferred_element_type=F32))
            for g in heads:
                for c in range(tq // rc):
                    rs = slice(c * rc, (c + 1) * rc)
                    r_prev = r_ref[g, rs, :]
                    w = jnp.exp2(z_ref[g, rs, :] - later_ref[g, rs, :] - _lane_tile(r_prev, reps))
                    if masked:
                        w = jnp.where(before(c), w, 0.0)
                    hi_ref[g, rs, :] = w.astype(BF16)
                    r_ref[g, rs, :] = r_prev + rsum_ref[g, rs, :]
            for g in heads:
                acc_ref[g] += jnp.dot(hi_ref[g], v_ref[ks, g * LANES:(g + 1) * LANES],
                                      preferred_element_type=F32)

        _side_effect_loop(0, n_kv - n_full, lambda t: step(n_kv - 1 - t, True))

        def more(c):
            return jnp.logical_and(c[0] >= 0, c[1] < -SB_LOG2_ZERO)

        def visit(c):
            step(c[0], False)
            return c[0] - 1, jnp.min(r_ref[...])

        lax.while_loop(more, visit, (n_full - 1, jnp.min(r_ref[...])))
        _emit_heads(o_ref, ssq_ref, rows, [acc_ref[g] for g in heads])

    _side_effect_loop(0, t_len // tq, q_block)


def _attention(kernel, q, kv_args, extra, *, batch, heads, hg, t_len, s_len, q_off, tq, tk, rc, scratch, name,
               merged=None, col_block=0, out_cols=None, span=1024):
    tq = min(tq, t_len)
    rc = min(rc, tq)
    span = min(span, t_len)
    nspan = t_len // span
    ngroups = heads // hg
    assert t_len % span == 0 and span % tq == 0 and tq % rc == 0 and s_len % tk == 0 and heads % hg == 0
    in_specs = [pl.BlockSpec((span, hg * (q.shape[-1] // heads)), lambda b, h, t: (b * nspan + t, h))]
    args = [q]
    for a, per_head in kv_args:
        if per_head:
            in_specs.append(pl.BlockSpec((s_len, hg * LANES), lambda b, h, t: (b, h)))
        else:
            in_specs.append(pl.BlockSpec((s_len, a.shape[-1]), lambda b, h, t: (b, 0)))
        args.append(a)
    for a in extra:
        in_specs.append(pl.BlockSpec(a.shape, lambda b, h, t: (0,) * a.ndim))
        args.append(a)
    aliases = {}
    if merged is not None:
        aliases[len(args)] = 0
        in_specs.append(pl.BlockSpec(memory_space=pl.ANY))
        args.append(merged)
    m = batch * t_len
    return pl.pallas_call(
        functools.partial(kernel, hg=hg, tq=tq, tk=tk, rc=rc, q_off=q_off, t_len=span, s_len=s_len),
        grid=(batch, ngroups, nspan),
        in_specs=in_specs,
        out_specs=[pl.BlockSpec((span, hg * LANES), lambda b, h, t: (b * nspan + t, col_block + h)),
                   pl.BlockSpec((None, span, LANES), lambda b, h, t: (h, b * nspan + t, 0))],
        out_shape=[jax.ShapeDtypeStruct((m, out_cols), BF16), jax.ShapeDtypeStruct((ngroups, m, LANES), F32)],
        scratch_shapes=scratch(hg, tq, tk),
        input_output_aliases=aliases,
        compiler_params=_params("parallel", "parallel", "parallel"),
        name=name,
    )(*args)


def mla_attention(q, k, kr, v, **kw):
    scratch = lambda hg, tq, tk: ([pltpu.VMEM((hg, tq, tk), F32), pltpu.VMEM((hg, tq, tk), BF16)]
                                  + [pltpu.VMEM((hg, tq, LANES), F32)] * 4)
    return _attention(_mla_kernel, q, [(k, True), (kr, False), (v, True)], [], scratch=scratch,
                      name="mla_attention", **kw)


def sb_attention(q, k, v, **kw):
    tk = kw["tk"]
    u = (lax.broadcasted_iota(jnp.int32, (tk, tk), 0)
         > lax.broadcasted_iota(jnp.int32, (tk, tk), 1)).astype(BF16)
    scratch = lambda hg, tq, tk: ([pltpu.VMEM((hg, tq, tk), F32)] * 2 + [pltpu.VMEM((hg, tq, tk), BF16)] * 2
                                  + [pltpu.VMEM((hg, tq, LANES), F32)] * 3)
    return _attention(_sb_kernel, q, [(k, True), (v, True)], [u], scratch=scratch,
                      name="sb_attention", **kw)


def _rope_table(pos, rope):
    half = rope // 2
    inv_freq = jnp.power(ROPE_THETA, -jnp.arange(half, dtype=F32) / half)
    ang = pos.astype(F32)[:, None] * inv_freq[None, :]
    cos, sin = jnp.cos(ang), jnp.sin(ang)
    return jnp.concatenate([cos, cos, -sin, sin], axis=1)


def _swap_halves(a, axis=-1):
    lo, hi = jnp.split(a, 2, axis=axis)
    return jnp.concatenate([hi, lo], axis=axis)


def _prep_layer(l, dims, w_in, w_q_b, w_kv_b, w_o, w_up, w_down, g_attn, g_q_nope, g_q_rope, g_k_rope,
                g_out_mla, g_out_sb, g_mlp):
    ql, kvl, rope, nope, vdim, heads, sbw = (dims[k] for k in
                                             ("ql", "kvl", "rope", "nope", "vdim", "heads", "sbw"))
    wi = cast_layer(w_in, l, gain=g_attn[l], tr=256, tc=w_in.shape[-1])
    d = wi.shape[0]
    kr = wi[:, ql + kvl:ql + kvl + rope]
    lat_w = ql + kvl + 2 * rope
    pad = (-lat_w) % MXU_DIM
    w_lat = jnp.concatenate([wi[:, :ql + kvl], kr, _swap_halves(kr), jnp.zeros((d, pad), BF16)], axis=1)
    o = ql + kvl + rope
    sb_scale = LOG2E * (sbw // dims["sbh"]) ** -0.5
    w_sq = (w_in[l, :, o:o + sbw] * (g_attn[l][:, None] * sb_scale)).astype(BF16)
    w_sk, w_sv = (wi[:, o + n * sbw:o + (n + 1) * sbw] for n in (1, 2))

    wq = w_q_b[l].reshape(ql, heads, nope + rope)
    wq_r = wq[:, :, nope:]
    wq = jnp.concatenate([wq[:, :, :nope], wq_r, _swap_halves(wq_r)], axis=2).reshape(ql, heads * 2 * LANES)

    wkv = w_kv_b[l].reshape(kvl, heads, nope + vdim)
    wk = wkv[:, :, :nope].reshape(kvl, heads * nope)
    wv = wkv[:, :, nope:].reshape(kvl, heads * vdim)

    scale = LOG2E * (nope + rope) ** -0.5
    qrow = (jnp.concatenate([g_q_nope[l], g_q_rope[l], _swap_halves(g_q_rope[l])]) * scale).reshape(1, -1)
    krow = jnp.concatenate([g_k_rope[l], _swap_halves(g_k_rope[l])]).reshape(1, -1)
    bf = lambda a: a.astype(BF16)
    return dict(w_lat=w_lat, w_sq=w_sq, w_sk=w_sk, w_sv=w_sv, wq=bf(wq), wk=bf(wk), wv=bf(wv),
                w_o=cast_layer(w_o, l, gain=jnp.concatenate([g_out_mla[l], g_out_sb[l]])),
                w_up=cast_layer(w_up, l, gain=g_mlp[l]), w_down=cast_layer(w_down, l),
                qrow=qrow, krow=krow)


def _layer(x, past, tab, lw, gains, dims, leaves, layer, depth, *, batch, t_len, q_off, tq_mla,
           tk_mla, tq_sb, tk_sb, tm_lat):
    ql, kvl, rope, heads, sbh = (dims[k] for k in ("ql", "kvl", "rope", "heads", "sbh"))
    prev = leaves if leaves is not None else (None,) * 4
    slot = lambda n: Stack(prev[n], layer, depth)
    xb, x_ssq = cast_ssq(x)
    (sq,) = matmul(xb, lw["w_sq"], [BF16], ssq=x_ssq)
    sk, skb = matmul(xb, lw["w_sk"], [F32, BF16], ssq=x_ssq, stack=slot(2))
    sv, svb = matmul(xb, lw["w_sv"], [F32, BF16], ssq=x_ssq, stack=slot(3))
    (lat,) = matmul(xb, lw["w_lat"], [F32], ssq=x_ssq)
    q_mla, ckv, ckvb, krope, kropeb = latent_post(
        lat, gains["g_q_a"].reshape(1, -1), lw["wq"], lw["qrow"], gains["g_kv_a"].reshape(1, -1),
        lw["krow"], tab, ql=ql, kvl=kvl, heads=heads, rope=rope, tm=tm_lat, stack_ckv=slot(0), stack_kr=slot(1))

    if past is None:
        s_len = t_len
        c_all, kr_all, sbk_all, sbv_all = ckvb, kropeb, skb, svb
    else:
        p_ckv, p_kr, p_sbk, p_sbv = past
        past_len = p_ckv.shape[1]
        s_len = -(-(past_len + t_len) // MXU_DIM) * MXU_DIM
        fill = s_len - past_len - t_len

        def rows(cached, new):
            width = new.shape[-1]
            cached = cached.reshape(batch, past_len, width).astype(BF16)
            return jnp.concatenate([cached, new.reshape(batch, t_len, width),
                                    jnp.zeros((batch, fill, width), BF16)], axis=1).reshape(batch * s_len, width)

        c_all = rows(p_ckv, ckvb)
        kr_all = rows(jnp.pad(p_kr, ((0, 0), (0, 0), (0, LANES - rope))), kropeb)
        sbk_all = rows(p_sbk, skb)
        sbv_all = rows(p_sbv, svb)

    k_mla, v_mla = kv_expand(c_all, lw["wk"], lw["wv"], gains["g_k_nope"].reshape(1, -1), heads=heads)
    hg = min(4, heads, sbh)
    assert heads == sbh
    common = dict(batch=batch, t_len=t_len, s_len=s_len, q_off=q_off, hg=hg, out_cols=(heads + sbh) * LANES)
    merged, ssq_mla = mla_attention(q_mla, k_mla, kr_all, v_mla, heads=heads, tq=tq_mla, tk=tk_mla, rc=64,
                                    **common)
    merged, ssq_sb = sb_attention(sq, sbk_all, sbv_all, heads=sbh, tq=tq_sb, tk=tk_sb, rc=128,
                                  merged=merged, col_block=heads // hg, **common)

    h, hb, h_ssq = matmul(merged, lw["w_o"], [F32, BF16], res=x, ssq=jnp.stack([ssq_mla, ssq_sb]),
                          emit_ssq=True, tk=heads * LANES)
    (u,) = matmul(hb, lw["w_up"], [BF16], act="relu2", ssq=h_ssq[None])
    (y,) = matmul(u, lw["w_down"], [F32], res=h, tk=4096)
    return y, (ckv, krope, sk, sv)


def kernel(x_prompt, x_sample, cache_mla_ckv, cache_mla_krope, cache_sb_k, cache_sb_v,
           g_attn, w_in, g_q_a, w_q_b, g_kv_a, w_kv_b, g_q_nope, g_q_rope, g_k_nope, g_k_rope,
           g_out_mla, g_out_sb, w_o, g_mlp, w_up, w_down):
    depth = w_in.shape[0]
    bp, tp, d = x_prompt.shape
    bs, ts, _ = x_sample.shape
    past_len = cache_mla_ckv.shape[2]
    sbh, sbd = cache_sb_k.shape[-2:]
    nope, rope = g_q_nope.shape[-1], g_q_rope.shape[-1]
    ql, kvl = g_q_a.shape[-1], g_kv_a.shape[-1]
    heads = w_q_b.shape[-1] // (nope + rope)
    vdim = w_kv_b.shape[-1] // heads - nope
    assert nope == LANES and vdim == LANES and sbd == LANES and 2 * rope == LANES
    dims = dict(ql=ql, kvl=kvl, rope=rope, nope=nope, vdim=vdim, heads=heads, sbh=sbh, sbw=sbh * sbd)

    tab_p = _rope_table(jnp.arange(tp, dtype=jnp.int32), rope)
    tab_s = _rope_table(past_len + jnp.arange(ts, dtype=jnp.int32), rope)

    hp = x_prompt.reshape(bp * tp, d)
    hs = x_sample.reshape(bs * ts, d)
    rows_p = rows_s = None
    for l in range(depth):
        lw = _prep_layer(l, dims, w_in, w_q_b, w_kv_b, w_o, w_up, w_down, g_attn, g_q_nope, g_q_rope, g_k_rope,
                         g_out_mla, g_out_sb, g_mlp)
        gains = dict(g_q_a=g_q_a[l], g_kv_a=g_kv_a[l], g_k_nope=g_k_nope[l])
        hp, rows_p = _layer(hp, None, tab_p, lw, gains, dims, rows_p, l, depth, batch=bp, t_len=tp, q_off=0,
                            tq_mla=512, tk_mla=512, tq_sb=256, tk_sb=256, tm_lat=256)
        past = (cache_mla_ckv[l], cache_mla_krope[l], cache_sb_k[l], cache_sb_v[l])
        hs, rows_s = _layer(hs, past, tab_s, lw, gains, dims, rows_s, l, depth, batch=bs, t_len=ts,
                            q_off=past_len, tq_mla=ts, tk_mla=MXU_DIM, tq_sb=ts, tk_sb=MXU_DIM, tm_lat=ts)

    def leaves(rows, b, t):
        shapes = ((kvl,), (rope,), (sbh, sbd), (sbh, sbd))
        return tuple(r.reshape(depth, b, t, *s) for r, s in zip(rows, shapes))

    return (hp.reshape(bp, tp, d), hs.reshape(bs, ts, d)) + leaves(rows_p, bp, tp) + leaves(rows_s, bs, ts)
```

```python
import functools

import jax
import jax.numpy as jnp
from jax import lax
from jax.experimental import pallas as pl
from jax.experimental.pallas import tpu as pltpu

EPS = 1e-6
NEG_INF = -1e30
CHUNK = 64
ROPE_THETA = 10000.0
LANES = 128
MXU_DIM = 256
VMEM_LIMIT = 60 * 1024 * 1024
LOG2E = 1.4426950408889634
SB_LOG2_ZERO = -160.0

F32 = jnp.float32
BF16 = jnp.bfloat16


def _pick(n, cap, mult=LANES):
    if n <= cap:
        return n
    best = None
    for d in range(mult, cap + 1, mult):
        if n % d == 0:
            best = d
    assert best is not None, (n, cap, mult)
    return best


def _params(*sem):
    return pltpu.CompilerParams(dimension_semantics=sem, vmem_limit_bytes=VMEM_LIMIT)


def _rms(x, g):
    return x * lax.rsqrt(jnp.mean(x * x, axis=-1, keepdims=True) + EPS) * g


def _lane_tile(x, reps):
    return x if reps == 1 else jnp.concatenate([x] * reps, axis=1)


def _row_ssq(x):
    return jnp.broadcast_to(jnp.sum(x * x, axis=1, keepdims=True), (x.shape[0], LANES))


def _side_effect_loop(lo, hi, fn):
    lax.fori_loop(lo, hi, lambda j, c: (fn(j), c)[1], 0)


class Stack:
    def __init__(self, prev, layer, depth):
        self.prev, self.layer, self.depth = prev, layer, depth


def _cast_ssq_kernel(x_ref, o_ref, ssq_ref):
    x = x_ref[...]
    o_ref[...] = x.astype(o_ref.dtype)
    ssq_ref[...] = _row_ssq(x)


def cast_ssq(x, tm=256):
    m, d = x.shape
    tm = _pick(m, tm, 16)
    return pl.pallas_call(
        _cast_ssq_kernel,
        grid=(m // tm,),
        in_specs=[pl.BlockSpec((tm, d), lambda i: (i, 0))],
        out_specs=[pl.BlockSpec((tm, d), lambda i: (i, 0)),
                   pl.BlockSpec((None, None, tm, LANES), lambda i: (0, 0, i, 0))],
        out_shape=[jax.ShapeDtypeStruct((m, d), BF16), jax.ShapeDtypeStruct((1, 1, m, LANES), F32)],
        compiler_params=_params("parallel"),
        name="cast_ssq",
    )(x)


def _cast_kernel(x_ref, *refs):
    o_ref = refs[-1]
    x = x_ref[...]
    if len(refs) == 2:
        x = x * refs[0][...]
    o_ref[...] = x.astype(o_ref.dtype)


def cast_layer(w, layer, gain=None, tr=512, tc=4096):
    _, r, c = w.shape
    tr = _pick(r, tr, 16)
    tc = _pick(c, tc)
    in_specs = [pl.BlockSpec((None, tr, tc), lambda i, j: (layer, i, j))]
    args = [w]
    if gain is not None:
        in_specs.append(pl.BlockSpec((tr, 1), lambda i, j: (i, 0)))
        args.append(gain.reshape(r, 1))
    return pl.pallas_call(
        _cast_kernel,
        grid=(r // tr, c // tc),
        in_specs=in_specs,
        out_specs=pl.BlockSpec((tr, tc), lambda i, j: (i, j)),
        out_shape=jax.ShapeDtypeStruct((r, c), BF16),
        compiler_params=_params("parallel", "parallel"),
        name="cast_layer",
    )(*args)


def _mm_kernel(*refs, nk, act, groups, parts, out_scale, has_res, n_aliased, n_out, emit_ssq, acc_in_out,
               w_out_major):
    x_ref, w_ref = refs[0], refs[1]
    p = 2
    ssq_in_ref = res_ref = ssq_out_ref = None
    if groups:
        ssq_in_ref = refs[p]
        p += 1
    if has_res:
        res_ref = refs[p]
        p += 1
    p += n_aliased
    out_refs = refs[p:p + n_out]
    p += n_out
    if emit_ssq:
        ssq_out_ref = refs[p]
        p += 1

    def finish(r):
        if act == "relu2":
            r = jnp.maximum(r, 0.0)
            r = r * r
        if has_res:
            r = r + res_ref[...]
        for o_ref in out_refs:
            o_ref[...] = r.astype(o_ref.dtype)
        if emit_ssq:
            ssq_out_ref[...] = _row_ssq(r)

    if groups:
        kg = x_ref.shape[1] // groups
        part = None
        for g in range(groups):
            ssq = ssq_in_ref[g, 0]
            for n in range(1, parts):
                ssq = ssq + ssq_in_ref[g, n]
            rinv = lax.rsqrt(ssq * (1.0 / kg) + EPS) * out_scale
            ks = slice(g * kg, (g + 1) * kg)
            if w_out_major:
                term = lax.dot_general(x_ref[:, ks], w_ref[:, ks], (((1,), (1,)), ((), ())),
                                       preferred_element_type=F32)
            else:
                term = jnp.dot(x_ref[:, ks], w_ref[ks, :], preferred_element_type=F32)
            term = term * _lane_tile(rinv, term.shape[1] // LANES)
            part = term if part is None else part + term
    else:
        assert not w_out_major
        part = jnp.dot(x_ref[...], w_ref[...], preferred_element_type=F32)
    if nk == 1:
        finish(part)
        return
    acc_ref = out_refs[0] if acc_in_out else refs[p]
    k = pl.program_id(2)

    @pl.when(k == 0)
    def _():
        acc_ref[...] = part

    @pl.when(jnp.logical_and(k > 0, k < nk - 1))
    def _():
        acc_ref[...] += part

    @pl.when(k == nk - 1)
    def _():
        finish(acc_ref[...] + part)


def matmul(x, w, outs, act=None, res=None, ssq=None, out_scale=1.0, emit_ssq=False, stack=None,
           w_out_major=False, tm=1024, tn=1024, tk=None):
    m, kdim = x.shape
    n = w.shape[0] if w_out_major else w.shape[1]
    tm = _pick(m, tm, 8)
    tn = _pick(n, tn)
    tk = kdim if tk is None else _pick(kdim, tk)
    nk = kdim // tk
    grid = (m // tm, n // tn, nk)
    in_specs = [pl.BlockSpec((tm, tk), lambda i, j, k: (i, k)),
                pl.BlockSpec((tn, tk), lambda i, j, k: (j, k)) if w_out_major
                else pl.BlockSpec((tk, tn), lambda i, j, k: (k, j))]
    args = [x, w]
    groups = parts = 0
    if ssq is not None:
        groups, parts = ssq.shape[:2]
        assert nk == 1 and kdim % (groups * LANES) == 0
        in_specs.append(pl.BlockSpec((groups, parts, tm, LANES), lambda i, j, k: (0, 0, i, 0)))
        args.append(ssq)
    else:
        assert out_scale == 1.0
    if res is not None:
        in_specs.append(pl.BlockSpec((tm, tn), lambda i, j, k: (i, j)))
        args.append(res)
    aliases = {}
    if stack is not None and stack.prev is not None:
        aliases[len(args)] = 0
        in_specs.append(pl.BlockSpec(memory_space=pl.ANY))
        args.append(stack.prev)
    out_specs, out_shapes = [], []
    for o, dtype in enumerate(outs):
        if o == 0 and stack is not None:
            layer = stack.layer
            out_specs.append(pl.BlockSpec((None, tm, tn), lambda i, j, k: (layer, i, j)))
            out_shapes.append(jax.ShapeDtypeStruct((stack.depth, m, n), dtype))
        else:
            out_specs.append(pl.BlockSpec((tm, tn), lambda i, j, k: (i, j)))
            out_shapes.append(jax.ShapeDtypeStruct((m, n), dtype))
    if emit_ssq:
        out_specs.append(pl.BlockSpec((None, tm, LANES), lambda i, j, k: (j, i, 0)))
        out_shapes.append(jax.ShapeDtypeStruct((n // tn, m, LANES), F32))
    acc_in_out = nk > 1 and act is None and stack is None and outs[0] == F32
    scratch = [pltpu.VMEM((tm, tn), F32)] if nk > 1 and not acc_in_out else []
    return pl.pallas_call(
        functools.partial(_mm_kernel, nk=nk, act=act, groups=groups, parts=parts, out_scale=out_scale,
                          w_out_major=w_out_major,
                          has_res=res is not None, n_aliased=len(aliases), n_out=len(outs), emit_ssq=emit_ssq,
                          acc_in_out=acc_in_out),
        grid=grid,
        in_specs=in_specs,
        out_specs=out_specs,
        out_shape=out_shapes,
        scratch_shapes=scratch,
        input_output_aliases=aliases,
        compiler_params=_params("parallel", "parallel", "arbitrary"),
        name="matmul",
    )(*args)


def _latent_kernel(lat_ref, gqa_ref, wq_ref, qrow_ref, gkva_ref, krow_ref, tab_ref, *refs,
                   ql, kvl, heads, rope):
    q_ref, ckv_ref, ckvb_ref, kr_ref, krb_ref = refs[-5:]
    lat = lat_ref[...]
    tab = tab_ref[...]

    def rotate(x, row):
        t = _rms(x, row) * tab
        return t + pltpu.roll(t, rope, axis=1)

    qn = _rms(lat[:, :ql], gqa_ref[...]).astype(BF16)
    q = jnp.dot(qn, wq_ref[...], preferred_element_type=F32)
    qrow = qrow_ref[...]
    for h in range(heads):
        base = h * 2 * LANES
        nope = _rms(q[:, base:base + LANES], qrow[:, :LANES])
        q_ref[:, base:base + LANES] = nope.astype(BF16)
        q_ref[:, base + LANES:base + 2 * LANES] = rotate(q[:, base + LANES:base + 2 * LANES],
                                                         qrow[:, LANES:]).astype(BF16)

    ckv = _rms(lat[:, ql:ql + kvl], gkva_ref[...])
    ckv_ref[...] = ckv
    ckvb_ref[...] = ckv.astype(BF16)

    kr = rotate(lat[:, ql + kvl:ql + kvl + LANES], krow_ref[...])
    kr_ref[...] = kr[:, :rope]
    lane = lax.broadcasted_iota(jnp.int32, kr.shape, 1)
    krb_ref[...] = jnp.where(lane < rope, kr, 0.0).astype(BF16)


def latent_post(lat, gqa, wq, qrow, gkva, krow, tab, *, ql, kvl, heads, rope, tm, stack_ckv, stack_kr):
    m = lat.shape[0]
    t = tab.shape[0]
    tm = min(tm, t)
    assert t % tm == 0 and m % tm == 0
    nt = t // tm
    layer, depth = stack_ckv.layer, stack_ckv.depth
    full = lambda a: pl.BlockSpec(a.shape, lambda i: (0,) * a.ndim)
    args = [lat, gqa, wq, qrow, gkva, krow, tab]
    in_specs = [pl.BlockSpec((tm, lat.shape[1]), lambda i: (i, 0)),
                full(gqa), full(wq), full(qrow), full(gkva), full(krow),
                pl.BlockSpec((tm, LANES), lambda i: (i % nt, 0))]
    aliases = {}
    for prev, out_idx in ((stack_ckv.prev, 1), (stack_kr.prev, 3)):
        if prev is not None:
            aliases[len(args)] = out_idx
            in_specs.append(pl.BlockSpec(memory_space=pl.ANY))
            args.append(prev)
    return pl.pallas_call(
        functools.partial(_latent_kernel, ql=ql, kvl=kvl, heads=heads, rope=rope),
        grid=(m // tm,),
        in_specs=in_specs,
        out_specs=[pl.BlockSpec((tm, heads * 2 * LANES), lambda i: (i, 0)),
                   pl.BlockSpec((None, tm, kvl), lambda i: (layer, i, 0)),
                   pl.BlockSpec((tm, kvl), lambda i: (i, 0)),
                   pl.BlockSpec((None, tm, rope), lambda i: (layer, i, 0)),
                   pl.BlockSpec((tm, LANES), lambda i: (i, 0))],
        out_shape=[jax.ShapeDtypeStruct((m, heads * 2 * LANES), BF16),
                   jax.ShapeDtypeStruct((depth, m, kvl), F32),
                   jax.ShapeDtypeStruct((m, kvl), BF16),
                   jax.ShapeDtypeStruct((depth, m, rope), F32),
                   jax.ShapeDtypeStruct((m, LANES), BF16)],
        input_output_aliases=aliases,
        compiler_params=_params("parallel"),
        name="latent_post",
    )(*args)


def _kv_expand_kernel(c_ref, wk_ref, wv_ref, g_ref, k_ref, v_ref, *, heads):
    c = c_ref[...]
    k = jnp.dot(c, wk_ref[...], preferred_element_type=F32)
    v = jnp.dot(c, wv_ref[...], preferred_element_type=F32)
    g = g_ref[...]
    for h in range(heads):
        sl = slice(h * LANES, (h + 1) * LANES)
        k_ref[:, sl] = _rms(k[:, sl], g).astype(BF16)
    v_ref[...] = v.astype(BF16)


def kv_expand(c, wk, wv, g, *, heads, tm=512):
    rows = c.shape[0]
    tm = _pick(rows, tm, 16)
    full = lambda a: pl.BlockSpec(a.shape, lambda i: (0,) * a.ndim)
    wide = jax.ShapeDtypeStruct((rows, heads * LANES), BF16)
    return pl.pallas_call(
        functools.partial(_kv_expand_kernel, heads=heads),
        grid=(rows // tm,),
        in_specs=[pl.BlockSpec((tm, c.shape[1]), lambda i: (i, 0)), full(wk), full(wv), full(g)],
        out_specs=[pl.BlockSpec((tm, heads * LANES), lambda i: (i, 0))] * 2,
        out_shape=[wide, wide],
        compiler_params=_params("parallel"),
        name="kv_expand",
    )(c, wk, wv, g)


def _emit_heads(o_ref, ssq_ref, rows, outs):
    ssq = None
    for g, o in enumerate(outs):
        o_ref[rows, g * LANES:(g + 1) * LANES] = o.astype(o_ref.dtype)
        ssq = _row_ssq(o) if ssq is None else ssq + _row_ssq(o)
    ssq_ref[rows, :] = ssq


def _mla_kernel(q_ref, k_ref, kr_ref, v_ref, *refs, hg, tq, tk, rc, q_off, t_len, s_len):
    o_ref, ssq_ref, s_ref, p_ref, m_ref, l_ref, a_ref, acc_ref = refs[-8:]
    nkb = s_len // tk
    heads = range(hg)

    def q_block(qi):
        rows = pl.ds(pl.multiple_of(qi * tq, tq), tq)
        q0 = q_off + pl.program_id(2) * t_len + qi * tq
        n_full = jnp.minimum(((q0 // CHUNK + 1) * CHUNK) // tk, nkb)
        n_kv = jnp.minimum((((q0 + tq - 1) // CHUNK + 1) * CHUNK + tk - 1) // tk, nkb)
        m_ref[...] = jnp.full(m_ref.shape, NEG_INF, F32)
        l_ref[...] = jnp.zeros(l_ref.shape, F32)
        acc_ref[...] = jnp.zeros(acc_ref.shape, F32)

        def step(j, masked):
            ks = pl.ds(pl.multiple_of(j * tk, tk), tk)
            kr = kr_ref[ks, :]
            for g in heads:
                k = jnp.concatenate([k_ref[ks, g * LANES:(g + 1) * LANES], kr], axis=1)
                q = q_ref[rows, g * 2 * LANES:(g + 1) * 2 * LANES]
                s_ref[g] = lax.dot_general(q, k, (((1,), (1,)), ((), ())),
                                           preferred_element_type=F32)
            for g in heads:
                for c in range(tq // rc):
                    rs = slice(c * rc, (c + 1) * rc)
                    s = s_ref[g, rs, :]
                    if masked:
                        qc = (q0 + c * rc + lax.broadcasted_iota(jnp.int32, (rc, 1), 0)) // CHUNK
                        kc = (j * tk + lax.broadcasted_iota(jnp.int32, (1, tk), 1)) // CHUNK
                        s = jnp.where(kc <= qc, s, NEG_INF)
                    m_prev = m_ref[g, rs, :]
                    m_next = jnp.maximum(m_prev, jnp.max(s, axis=1, keepdims=True))
                    alpha = jnp.exp2(m_prev - m_next)
                    p = jnp.exp2(s - _lane_tile(m_next, tk // LANES))
                    l_ref[g, rs, :] = alpha * l_ref[g, rs, :] + jnp.sum(p, axis=1, keepdims=True)
                    m_ref[g, rs, :] = m_next
                    a_ref[g, rs, :] = alpha
                    p_ref[g, rs, :] = p.astype(BF16)
            for g in heads:
                acc_ref[g] = a_ref[g] * acc_ref[g] + jnp.dot(p_ref[g], v_ref[ks, g * LANES:(g + 1) * LANES],
                                                             preferred_element_type=F32)

        _side_effect_loop(0, n_full, lambda j: step(j, False))
        _side_effect_loop(n_full, n_kv, lambda j: step(j, True))
        _emit_heads(o_ref, ssq_ref, rows, [acc_ref[g] / l_ref[g] for g in heads])

    _side_effect_loop(0, t_len // tq, q_block)


def _sb_kernel(q_ref, k_ref, v_ref, u_ref, *refs, hg, tq, tk, rc, q_off, t_len, s_len):
    o_ref, ssq_ref, z_ref, later_ref, hi_ref, lo_ref, r_ref, rsum_ref, acc_ref = refs[-9:]
    nkb = s_len // tk
    reps = tk // LANES
    heads = range(hg)

    def q_block(qi):
        rows = pl.ds(pl.multiple_of(qi * tq, tq), tq)
        q0 = q_off + pl.program_id(2) * t_len + qi * tq
        n_full = jnp.minimum(q0 // tk, nkb)
        n_kv = jnp.minimum((q0 + tq - 1 + tk - 1) // tk, nkb)
        r_ref[...] = jnp.zeros(r_ref.shape, F32)
        acc_ref[...] = jnp.zeros(acc_ref.shape, F32)

        def step(j, masked):
            ks = pl.ds(pl.multiple_of(j * tk, tk), tk)

            def before(c):
                qp = q0 + c * rc + lax.broadcasted_iota(jnp.int32, (rc, 1), 0)
                kp = j * tk + lax.broadcasted_iota(jnp.int32, (1, tk), 1)
                return kp < qp

            for g in heads:
                hs = slice(g * LANES, (g + 1) * LANES)
                z_ref[g] = lax.dot_general(q_ref[rows, hs], k_ref[ks, hs], (((1,), (1,)), ((), ())),
                                           preferred_element_type=F32)
            for g in heads:
                for c in range(tq // rc):
                    rs = slice(c * rc, (c + 1) * rc)
                    z = z_ref[g, rs, :]
                    fail = jnp.maximum(z, 0.0) + jnp.log(1.0 + jnp.exp2(-jnp.abs(z))) * LOG2E
                    z_ref[g, rs, :] = z - fail
                    if masked:
                        fail = jnp.where(before(c), fail, 0.0)
                    hi = fail.astype(BF16)
                    hi_ref[g, rs, :] = hi
                    lo_ref[g, rs, :] = (fail - hi.astype(F32)).astype(BF16)
                    rsum_ref[g, rs, :] = jnp.broadcast_to(jnp.sum(fail, axis=1, keepdims=True), (rc, LANES))
            u = u_ref[...]
            for g in heads:
                later_ref[g] = (jnp.dot(hi_ref[g], u, preferred_element_type=F32)
                                + jnp.dot(lo_ref[g], u, preferred_element_type=F32))
            for g in heads:
                for c in range(tq // rc):
                    rs = slice(c * rc, (c + 1) * rc)
                    r_prev = r_ref[g, rs, :]
                    w = jnp.exp2(z_ref[g, rs, :] - later_ref[g, rs, :] - _lane_tile(r_prev, reps))
                    if masked:
                        w = jnp.where(before(c), w, 0.0)
                    hi_ref[g, rs, :] = w.astype(BF16)
                    r_ref[g, rs, :] = r_prev + rsum_ref[g, rs, :]
            for g in heads:
                acc_ref[g] += jnp.dot(hi_ref[g], v_ref[ks, g * LANES:(g + 1) * LANES],
                                      preferred_element_type=F32)

        _side_effect_loop(0, n_kv - n_full, lambda t: step(n_kv - 1 - t, True))

        def more(c):
            return jnp.logical_and(c[0] >= 0, c[1] < -SB_LOG2_ZERO)

        def visit(c):
            step(c[0], False)
            return c[0] - 1, jnp.min(r_ref[...])

        lax.while_loop(more, visit, (n_full - 1, jnp.min(r_ref[...])))
        _emit_heads(o_ref, ssq_ref, rows, [acc_ref[g] for g in heads])

    _side_effect_loop(0, t_len // tq, q_block)


def _attention(kernel, q, kv_args, extra, *, batch, heads, hg, t_len, s_len, q_off, tq, tk, rc, scratch, name,
               prev=None, half=0, span=1024):
    tq = min(tq, t_len)
    rc = min(rc, tq)
    span = min(span, t_len)
    nspan = t_len // span
    ngroups = heads // hg
    assert t_len % span == 0 and span % tq == 0 and tq % rc == 0 and s_len % tk == 0 and heads % hg == 0
    in_specs = [pl.BlockSpec((span, hg * (q.shape[-1] // heads)), lambda b, h, t: (b * nspan + t, h))]
    args = [q]
    for a, per_head in kv_args:
        if per_head:
            in_specs.append(pl.BlockSpec((s_len, hg * LANES), lambda b, h, t: (b, h)))
        else:
            in_specs.append(pl.BlockSpec((s_len, a.shape[-1]), lambda b, h, t: (b, 0)))
        args.append(a)
    for a in extra:
        in_specs.append(pl.BlockSpec(a.shape, lambda b, h, t, nd=a.ndim: (0,) * nd))
        args.append(a)
    aliases = {}
    if prev is not None:
        for out_idx, a in enumerate(prev):
            aliases[len(args)] = out_idx
            in_specs.append(pl.BlockSpec(memory_space=pl.ANY))
            args.append(a)
    m = batch * t_len
    return pl.pallas_call(
        functools.partial(kernel, hg=hg, tq=tq, tk=tk, rc=rc, q_off=q_off, t_len=span, s_len=s_len),
        grid=(batch, ngroups, nspan),
        in_specs=in_specs,
        out_specs=[pl.BlockSpec((span, hg * LANES), lambda b, h, t: (b * nspan + t, half * ngroups + h)),
                   pl.BlockSpec((None, None, span, LANES), lambda b, h, t: (half, h, b * nspan + t, 0))],
        out_shape=[jax.ShapeDtypeStruct((m, 2 * heads * LANES), BF16),
                   jax.ShapeDtypeStruct((2, ngroups, m, LANES), F32)],
        scratch_shapes=scratch(hg, tq, tk),
        input_output_aliases=aliases,
        compiler_params=_params("parallel", "parallel", "parallel"),
        name=name,
    )(*args)


def mla_attention(q, k, kr, v, **kw):
    scratch = lambda hg, tq, tk: ([pltpu.VMEM((hg, tq, tk), F32), pltpu.VMEM((hg, tq, tk), BF16)]
                                  + [pltpu.VMEM((hg, tq, LANES), F32)] * 4)
    return _attention(_mla_kernel, q, [(k, True), (kr, False), (v, True)], [], scratch=scratch,
                      name="mla_attention", **kw)


def sb_attention(q, k, v, **kw):
    tk = kw["tk"]
    u = (lax.broadcasted_iota(jnp.int32, (tk, tk), 0)
         > lax.broadcasted_iota(jnp.int32, (tk, tk), 1)).astype(BF16)
    scratch = lambda hg, tq, tk: ([pltpu.VMEM((hg, tq, tk), F32)] * 2 + [pltpu.VMEM((hg, tq, tk), BF16)] * 2
                                  + [pltpu.VMEM((hg, tq, LANES), F32)] * 3)
    return _attention(_sb_kernel, q, [(k, True), (v, True)], [u], scratch=scratch,
                      name="sb_attention", **kw)


def _rope_table(pos, rope):
    half = rope // 2
    inv_freq = jnp.power(ROPE_THETA, -jnp.arange(half, dtype=F32) / half)
    ang = pos.astype(F32)[:, None] * inv_freq[None, :]
    cos, sin = jnp.cos(ang), jnp.sin(ang)
    return jnp.concatenate([cos, cos, -sin, sin], axis=1)


def _swap_halves(a, axis=-1):
    lo, hi = jnp.split(a, 2, axis=axis)
    return jnp.concatenate([hi, lo], axis=axis)


def _prep_layer(l, dims, w_in, w_q_b, w_kv_b, w_o, w_up, w_down, g_attn, g_q_nope, g_q_rope, g_k_rope,
                g_out_mla, g_out_sb, g_mlp):
    ql, kvl, rope, nope, vdim, heads, sbw = (dims[k] for k in
                                             ("ql", "kvl", "rope", "nope", "vdim", "heads", "sbw"))
    wt = jnp.swapaxes(w_in, 1, 2)[l]
    d = wt.shape[1]
    fold = lambda rows: (rows * g_attn[l][None, :]).astype(BF16)
    kr = wt[ql + kvl:ql + kvl + rope]
    lat_w = ql + kvl + 2 * rope
    pad = (-lat_w) % MXU_DIM
    w_lat = fold(jnp.concatenate([wt[:ql + kvl], kr, _swap_halves(kr, axis=0), jnp.zeros((pad, d), F32)], axis=0))
    o = ql + kvl + rope
    w_sq, w_sk, w_sv = (fold(wt[o + n * sbw:o + (n + 1) * sbw]) for n in range(3))

    wq = w_q_b[l].reshape(ql, heads, nope + rope)
    wq_r = wq[:, :, nope:]
    wq = jnp.concatenate([wq[:, :, :nope], wq_r, _swap_halves(wq_r)], axis=2).reshape(ql, heads * 2 * LANES)

    wkv = w_kv_b[l].reshape(kvl, heads, nope + vdim)
    wk = wkv[:, :, :nope].reshape(kvl, heads * nope)
    wv = wkv[:, :, nope:].reshape(kvl, heads * vdim)

    scale = LOG2E * (nope + rope) ** -0.5
    qrow = (jnp.concatenate([g_q_nope[l], g_q_rope[l], _swap_halves(g_q_rope[l])]) * scale).reshape(1, -1)
    krow = jnp.concatenate([g_k_rope[l], _swap_halves(g_k_rope[l])]).reshape(1, -1)
    bf = lambda a: a.astype(BF16)
    return dict(w_lat=w_lat, w_sq=w_sq, w_sk=w_sk, w_sv=w_sv, wq=bf(wq), wk=bf(wk), wv=bf(wv),
                w_o=cast_layer(w_o, l, gain=jnp.concatenate([g_out_mla[l], g_out_sb[l]])),
                w_up=cast_layer(w_up, l, gain=g_mlp[l]), w_down=cast_layer(w_down, l),
                qrow=qrow, krow=krow)


def _layer(x, past, tab, lw, gains, dims, leaves, layer, depth, *, batch, t_len, q_off, tq_mla,
           tk_mla, tq_sb, tk_sb, tm_lat):
    ql, kvl, rope, heads, sbh = (dims[k] for k in ("ql", "kvl", "rope", "heads", "sbh"))
    prev = leaves if leaves is not None else (None,) * 4
    slot = lambda n: Stack(prev[n], layer, depth)
    xb, x_ssq = cast_ssq(x)
    sb_scale = LOG2E * (dims["sbw"] // sbh) ** -0.5
    in_proj = functools.partial(matmul, xb, ssq=x_ssq, w_out_major=True)
    (sq,) = in_proj(lw["w_sq"], [BF16], out_scale=sb_scale)
    sk, skb = in_proj(lw["w_sk"], [F32, BF16], stack=slot(2))
    sv, svb = in_proj(lw["w_sv"], [F32, BF16], stack=slot(3))
    (lat,) = in_proj(lw["w_lat"], [F32])
    q_mla, ckv, ckvb, krope, kropeb = latent_post(
        lat, gains["g_q_a"].reshape(1, -1), lw["wq"], lw["qrow"], gains["g_kv_a"].reshape(1, -1),
        lw["krow"], tab, ql=ql, kvl=kvl, heads=heads, rope=rope, tm=tm_lat, stack_ckv=slot(0), stack_kr=slot(1))

    if past is None:
        s_len = t_len
        c_all, kr_all, sbk_all, sbv_all = ckvb, kropeb, skb, svb
    else:
        p_ckv, p_kr, p_sbk, p_sbv = past
        past_len = p_ckv.shape[1]
        s_len = -(-(past_len + t_len) // MXU_DIM) * MXU_DIM
        fill = s_len - past_len - t_len

        def rows(cached, new):
            width = new.shape[-1]
            return jnp.concatenate([cached.astype(BF16), new.reshape(batch, t_len, width),
                                    jnp.zeros((batch, fill, width), BF16)], axis=1).reshape(batch * s_len, width)

        c_all = rows(p_ckv, ckvb)
        kr_all = rows(jnp.pad(p_kr, ((0, 0), (0, 0), (0, LANES - rope))), kropeb)
        sbk_all = rows(p_sbk, skb)
        sbv_all = rows(p_sbv, svb)

    k_mla, v_mla = kv_expand(c_all, lw["wk"], lw["wv"], gains["g_k_nope"].reshape(1, -1), heads=heads)
    hg = min(4, heads, sbh)
    assert heads == sbh
    common = dict(batch=batch, t_len=t_len, s_len=s_len, q_off=q_off, hg=hg)
    mixed = mla_attention(q_mla, k_mla, kr_all, v_mla, heads=heads, tq=tq_mla, tk=tk_mla, rc=64, **common)
    merged, mix_ssq = sb_attention(sq, sbk_all, sbv_all, heads=sbh, tq=tq_sb, tk=tk_sb, rc=128, prev=mixed,
                                   half=1, **common)

    h, hb, h_ssq = matmul(merged, lw["w_o"], [F32, BF16], res=x, ssq=mix_ssq, emit_ssq=True, tn=512)
    (u,) = matmul(hb, lw["w_up"], [BF16], act="relu2", ssq=h_ssq[None])
    (y,) = matmul(u, lw["w_down"], [F32], res=h, tk=4096)
    return y, (ckv, krope, sk, sv)


def kernel(x_prompt, x_sample, cache_mla_ckv, cache_mla_krope, cache_sb_k, cache_sb_v,
           g_attn, w_in, g_q_a, w_q_b, g_kv_a, w_kv_b, g_q_nope, g_q_rope, g_k_nope, g_k_rope,
           g_out_mla, g_out_sb, w_o, g_mlp, w_up, w_down):
    depth = w_in.shape[0]
    bp, tp, d = x_prompt.shape
    bs, ts, _ = x_sample.shape
    past_len = cache_mla_ckv.shape[2]
    sbh, sbd = cache_sb_k.shape[-2:]
    nope, rope = g_q_nope.shape[-1], g_q_rope.shape[-1]
    ql, kvl = g_q_a.shape[-1], g_kv_a.shape[-1]
    heads = w_q_b.shape[-1] // (nope + rope)
    vdim = w_kv_b.shape[-1] // heads - nope
    assert nope == LANES and vdim == LANES and sbd == LANES and 2 * rope == LANES
    dims = dict(ql=ql, kvl=kvl, rope=rope, nope=nope, vdim=vdim, heads=heads, sbh=sbh, sbw=sbh * sbd)

    tab_p = _rope_table(jnp.arange(tp, dtype=jnp.int32), rope)
    tab_s = _rope_table(past_len + jnp.arange(ts, dtype=jnp.int32), rope)

    hp = x_prompt.reshape(bp * tp, d)
    hs = x_sample.reshape(bs * ts, d)
    sb_k_rows, sb_v_rows = lax.optimization_barrier(
        (cache_sb_k.reshape(depth, bs, past_len, sbh * sbd), cache_sb_v.reshape(depth, bs, past_len, sbh * sbd)))
    rows_p = rows_s = None
    for l in range(depth):
        lw = _prep_layer(l, dims, w_in, w_q_b, w_kv_b, w_o, w_up, w_down, g_attn, g_q_nope, g_q_rope, g_k_rope,
                         g_out_mla, g_out_sb, g_mlp)
        gains = dict(g_q_a=g_q_a[l], g_kv_a=g_kv_a[l], g_k_nope=g_k_nope[l])
        hp, rows_p = _layer(hp, None, tab_p, lw, gains, dims, rows_p, l, depth, batch=bp, t_len=tp, q_off=0,
                            tq_mla=512, tk_mla=512, tq_sb=256, tk_sb=256, tm_lat=256)
        past = (cache_mla_ckv[l], cache_mla_krope[l], sb_k_rows[l], sb_v_rows[l])
        hs, rows_s = _layer(hs, past, tab_s, lw, gains, dims, rows_s, l, depth, batch=bs, t_len=ts,
                            q_off=past_len, tq_mla=ts, tk_mla=MXU_DIM, tq_sb=ts, tk_sb=MXU_DIM, tm_lat=ts)

    def leaves(rows, b, t):
        shapes = ((kvl,), (rope,), (sbh, sbd), (sbh, sbd))
        return tuple(r.reshape(depth, b, t, *s) for r, s in zip(rows, shapes))

    return (hp.reshape(bp, tp, d), hs.reshape(bs, ts, d)) + leaves(rows_p, bp, tp) + leaves(rows_s, bs, ts)
```

```python
import functools

import jax
import jax.numpy as jnp
from jax import lax
from jax.experimental import pallas as pl
from jax.experimental.pallas import tpu as pltpu

EPS = 1e-6
NEG_INF = -1e30
CHUNK = 64
ROPE_THETA = 10000.0
LANES = 128
MXU_DIM = 256
VMEM_LIMIT = 60 * 1024 * 1024
LOG2E = 1.4426950408889634
SB_LOG2_ZERO = -160.0

F32 = jnp.float32
BF16 = jnp.bfloat16


def _pick(n, cap, mult=LANES):
    if n <= cap:
        return n
    best = None
    for d in range(mult, cap + 1, mult):
        if n % d == 0:
            best = d
    assert best is not None, (n, cap, mult)
    return best


def _params(*sem):
    return pltpu.CompilerParams(dimension_semantics=sem, vmem_limit_bytes=VMEM_LIMIT)


def _rms(x, g):
    return x * lax.rsqrt(jnp.mean(x * x, axis=-1, keepdims=True) + EPS) * g


def _lane_tile(x, reps):
    return x if reps == 1 else jnp.concatenate([x] * reps, axis=1)


def _row_ssq(x):
    return jnp.broadcast_to(jnp.sum(x * x, axis=1, keepdims=True), (x.shape[0], LANES))


def _side_effect_loop(lo, hi, fn):
    lax.fori_loop(lo, hi, lambda j, c: (fn(j), c)[1], 0)


class Stack:
    def __init__(self, prev, layer, depth):
        self.prev, self.layer, self.depth = prev, layer, depth


def _cast_ssq_kernel(x_ref, o_ref, ssq_ref):
    x = x_ref[...]
    o_ref[...] = x.astype(o_ref.dtype)
    ssq_ref[...] = _row_ssq(x)


def cast_ssq(x, tm=256):
    m, d = x.shape
    tm = _pick(m, tm, 16)
    return pl.pallas_call(
        _cast_ssq_kernel,
        grid=(m // tm,),
        in_specs=[pl.BlockSpec((tm, d), lambda i: (i, 0))],
        out_specs=[pl.BlockSpec((tm, d), lambda i: (i, 0)),
                   pl.BlockSpec((None, None, tm, LANES), lambda i: (0, 0, i, 0))],
        out_shape=[jax.ShapeDtypeStruct((m, d), BF16), jax.ShapeDtypeStruct((1, 1, m, LANES), F32)],
        compiler_params=_params("parallel"),
        name="cast_ssq",
    )(x)


def _cast_kernel(x_ref, *refs):
    o_ref = refs[-1]
    x = x_ref[...]
    if len(refs) == 2:
        x = x * refs[0][...]
    o_ref[...] = x.astype(o_ref.dtype)


def cast_layer(w, layer, gain=None, tr=512, tc=4096):
    _, r, c = w.shape
    tr = _pick(r, tr, 16)
    tc = _pick(c, tc)
    in_specs = [pl.BlockSpec((None, tr, tc), lambda i, j: (layer, i, j))]
    args = [w]
    if gain is not None:
        in_specs.append(pl.BlockSpec((tr, 1), lambda i, j: (i, 0)))
        args.append(gain.reshape(r, 1))
    return pl.pallas_call(
        _cast_kernel,
        grid=(r // tr, c // tc),
        in_specs=in_specs,
        out_specs=pl.BlockSpec((tr, tc), lambda i, j: (i, j)),
        out_shape=jax.ShapeDtypeStruct((r, c), BF16),
        compiler_params=_params("parallel", "parallel"),
        name="cast_layer",
    )(*args)


def _mm_kernel(*refs, nk, act, groups, parts, out_scale, has_res, n_aliased, n_out, emit_ssq, acc_in_out,
               w_out_major):
    x_ref, w_ref = refs[0], refs[1]
    p = 2
    ssq_in_ref = res_ref = ssq_out_ref = None
    if groups:
        ssq_in_ref = refs[p]
        p += 1
    if has_res:
        res_ref = refs[p]
        p += 1
    p += n_aliased
    out_refs = refs[p:p + n_out]
    p += n_out
    if emit_ssq:
        ssq_out_ref = refs[p]
        p += 1

    def finish(r):
        if act == "relu2":
            r = jnp.maximum(r, 0.0)
            r = r * r
        if has_res:
            r = r + res_ref[...]
        for o_ref in out_refs:
            o_ref[...] = r.astype(o_ref.dtype)
        if emit_ssq:
            ssq_out_ref[...] = _row_ssq(r)

    if groups:
        kg = x_ref.shape[1] // groups
        part = None
        for g in range(groups):
            ssq = ssq_in_ref[g, 0]
            for n in range(1, parts):
                ssq = ssq + ssq_in_ref[g, n]
            rinv = lax.rsqrt(ssq * (1.0 / kg) + EPS) * out_scale
            ks = slice(g * kg, (g + 1) * kg)
            if w_out_major:
                term = lax.dot_general(x_ref[:, ks], w_ref[:, ks], (((1,), (1,)), ((), ())),
                                       preferred_element_type=F32)
            else:
                term = jnp.dot(x_ref[:, ks], w_ref[ks, :], preferred_element_type=F32)
            term = term * _lane_tile(rinv, term.shape[1] // LANES)
            part = term if part is None else part + term
    else:
        assert not w_out_major
        part = jnp.dot(x_ref[...], w_ref[...], preferred_element_type=F32)
    if nk == 1:
        finish(part)
        return
    acc_ref = out_refs[0] if acc_in_out else refs[p]
    k = pl.program_id(2)

    @pl.when(k == 0)
    def _():
        acc_ref[...] = part

    @pl.when(jnp.logical_and(k > 0, k < nk - 1))
    def _():
        acc_ref[...] += part

    @pl.when(k == nk - 1)
    def _():
        finish(acc_ref[...] + part)


def matmul(x, w, outs, act=None, res=None, ssq=None, out_scale=1.0, emit_ssq=False, stack=None,
           w_out_major=False, tm=1024, tn=1024, tk=None):
    m, kdim = x.shape
    n = w.shape[0] if w_out_major else w.shape[1]
    tm = _pick(m, tm, 8)
    tn = _pick(n, tn)
    tk = kdim if tk is None else _pick(kdim, tk)
    nk = kdim // tk
    grid = (m // tm, n // tn, nk)
    in_specs = [pl.BlockSpec((tm, tk), lambda i, j, k: (i, k)),
                pl.BlockSpec((tn, tk), lambda i, j, k: (j, k)) if w_out_major
                else pl.BlockSpec((tk, tn), lambda i, j, k: (k, j))]
    args = [x, w]
    groups = parts = 0
    if ssq is not None:
        groups, parts = ssq.shape[:2]
        assert nk == 1 and kdim % (groups * LANES) == 0
        in_specs.append(pl.BlockSpec((groups, parts, tm, LANES), lambda i, j, k: (0, 0, i, 0)))
        args.append(ssq)
    else:
        assert out_scale == 1.0
    if res is not None:
        in_specs.append(pl.BlockSpec((tm, tn), lambda i, j, k: (i, j)))
        args.append(res)
    aliases = {}
    if stack is not None and stack.prev is not None:
        aliases[len(args)] = 0
        in_specs.append(pl.BlockSpec(memory_space=pl.ANY))
        args.append(stack.prev)
    out_specs, out_shapes = [], []
    for o, dtype in enumerate(outs):
        if o == 0 and stack is not None:
            layer = stack.layer
            out_specs.append(pl.BlockSpec((None, tm, tn), lambda i, j, k: (layer, i, j)))
            out_shapes.append(jax.ShapeDtypeStruct((stack.depth, m, n), dtype))
        else:
            out_specs.append(pl.BlockSpec((tm, tn), lambda i, j, k: (i, j)))
            out_shapes.append(jax.ShapeDtypeStruct((m, n), dtype))
    if emit_ssq:
        out_specs.append(pl.BlockSpec((None, tm, LANES), lambda i, j, k: (j, i, 0)))
        out_shapes.append(jax.ShapeDtypeStruct((n // tn, m, LANES), F32))
    acc_in_out = nk > 1 and act is None and stack is None and outs[0] == F32
    scratch = [pltpu.VMEM((tm, tn), F32)] if nk > 1 and not acc_in_out else []
    return pl.pallas_call(
        functools.partial(_mm_kernel, nk=nk, act=act, groups=groups, parts=parts, out_scale=out_scale,
                          w_out_major=w_out_major,
                          has_res=res is not None, n_aliased=len(aliases), n_out=len(outs), emit_ssq=emit_ssq,
                          acc_in_out=acc_in_out),
        grid=grid,
        in_specs=in_specs,
        out_specs=out_specs,
        out_shape=out_shapes,
        scratch_shapes=scratch,
        input_output_aliases=aliases,
        compiler_params=_params("parallel", "parallel", "arbitrary"),
        name="matmul",
    )(*args)


def _latent_kernel(lat_ref, gqa_ref, wq_ref, qrow_ref, gkva_ref, krow_ref, tab_ref, *refs,
                   ql, kvl, heads, rope):
    q_ref, ckv_ref, ckvb_ref, kr_ref, krb_ref = refs[-5:]
    lat = lat_ref[...]
    tab = tab_ref[...]

    def rotate(x, row):
        t = _rms(x, row) * tab
        return t + pltpu.roll(t, rope, axis=1)

    qn = _rms(lat[:, :ql], gqa_ref[...]).astype(BF16)
    q = jnp.dot(qn, wq_ref[...], preferred_element_type=F32)
    qrow = qrow_ref[...]
    for h in range(heads):
        base = h * 2 * LANES
        nope = _rms(q[:, base:base + LANES], qrow[:, :LANES])
        q_ref[:, base:base + LANES] = nope.astype(BF16)
        q_ref[:, base + LANES:base + 2 * LANES] = rotate(q[:, base + LANES:base + 2 * LANES],
                                                         qrow[:, LANES:]).astype(BF16)

    ckv = _rms(lat[:, ql:ql + kvl], gkva_ref[...])
    ckv_ref[...] = ckv
    ckvb_ref[...] = ckv.astype(BF16)

    kr = rotate(lat[:, ql + kvl:ql + kvl + LANES], krow_ref[...])
    kr_ref[...] = kr[:, :rope]
    lane = lax.broadcasted_iota(jnp.int32, kr.shape, 1)
    krb_ref[...] = jnp.where(lane < rope, kr, 0.0).astype(BF16)


def latent_post(lat, gqa, wq, qrow, gkva, krow, tab, *, ql, kvl, heads, rope, tm, stack_ckv, stack_kr):
    m = lat.shape[0]
    t = tab.shape[0]
    tm = min(tm, t)
    assert t % tm == 0 and m % tm == 0
    nt = t // tm
    layer, depth = stack_ckv.layer, stack_ckv.depth
    full = lambda a: pl.BlockSpec(a.shape, lambda i: (0,) * a.ndim)
    args = [lat, gqa, wq, qrow, gkva, krow, tab]
    in_specs = [pl.BlockSpec((tm, lat.shape[1]), lambda i: (i, 0)),
                full(gqa), full(wq), full(qrow), full(gkva), full(krow),
                pl.BlockSpec((tm, LANES), lambda i: (i % nt, 0))]
    aliases = {}
    for prev, out_idx in ((stack_ckv.prev, 1), (stack_kr.prev, 3)):
        if prev is not None:
            aliases[len(args)] = out_idx
            in_specs.append(pl.BlockSpec(memory_space=pl.ANY))
            args.append(prev)
    return pl.pallas_call(
        functools.partial(_latent_kernel, ql=ql, kvl=kvl, heads=heads, rope=rope),
        grid=(m // tm,),
        in_specs=in_specs,
        out_specs=[pl.BlockSpec((tm, heads * 2 * LANES), lambda i: (i, 0)),
                   pl.BlockSpec((None, tm, kvl), lambda i: (layer, i, 0)),
                   pl.BlockSpec((tm, kvl), lambda i: (i, 0)),
                   pl.BlockSpec((None, tm, rope), lambda i: (layer, i, 0)),
                   pl.BlockSpec((tm, LANES), lambda i: (i, 0))],
        out_shape=[jax.ShapeDtypeStruct((m, heads * 2 * LANES), BF16),
                   jax.ShapeDtypeStruct((depth, m, kvl), F32),
                   jax.ShapeDtypeStruct((m, kvl), BF16),
                   jax.ShapeDtypeStruct((depth, m, rope), F32),
                   jax.ShapeDtypeStruct((m, LANES), BF16)],
        input_output_aliases=aliases,
        compiler_params=_params("parallel"),
        name="latent_post",
    )(*args)


def _kv_expand_kernel(c_ref, wk_ref, wv_ref, g_ref, k_ref, v_ref, *, heads):
    c = c_ref[...]
    k = jnp.dot(c, wk_ref[...], preferred_element_type=F32)
    v = jnp.dot(c, wv_ref[...], preferred_element_type=F32)
    g = g_ref[...]
    for h in range(heads):
        sl = slice(h * LANES, (h + 1) * LANES)
        k_ref[:, sl] = _rms(k[:, sl], g).astype(BF16)
    v_ref[...] = v.astype(BF16)


def kv_expand(c, wk, wv, g, *, heads, tm=512):
    rows = c.shape[0]
    tm = _pick(rows, tm, 16)
    full = lambda a: pl.BlockSpec(a.shape, lambda i: (0,) * a.ndim)
    wide = jax.ShapeDtypeStruct((rows, heads * LANES), BF16)
    return pl.pallas_call(
        functools.partial(_kv_expand_kernel, heads=heads),
        grid=(rows // tm,),
        in_specs=[pl.BlockSpec((tm, c.shape[1]), lambda i: (i, 0)), full(wk), full(wv), full(g)],
        out_specs=[pl.BlockSpec((tm, heads * LANES), lambda i: (i, 0))] * 2,
        out_shape=[wide, wide],
        compiler_params=_params("parallel"),
        name="kv_expand",
    )(c, wk, wv, g)


def _staggered(stages, hg):
    for t in range(hg + len(stages) - 1):
        for n, stage in enumerate(stages):
            if 0 <= t - n < hg:
                stage(t - n)


def _emit_heads(o_ref, ssq_ref, rows, outs):
    ssq = None
    for g, o in enumerate(outs):
        o_ref[rows, g * LANES:(g + 1) * LANES] = o.astype(o_ref.dtype)
        ssq = _row_ssq(o) if ssq is None else ssq + _row_ssq(o)
    ssq_ref[0, rows, :] = ssq
    for n in range(1, ssq_ref.shape[0]):
        ssq_ref[n, rows, :] = jnp.zeros_like(ssq)


def _mla_kernel(q_ref, k_ref, kr_ref, v_ref, *refs, hg, tq, tk, rc, q_off, t_len, s_len):
    o_ref, ssq_ref, s_ref, p_ref, m_ref, l_ref, a_ref, acc_ref = refs[-8:]
    nkb = s_len // tk
    heads = range(hg)

    def q_block(qi):
        rows = pl.ds(pl.multiple_of(qi * tq, tq), tq)
        q0 = q_off + pl.program_id(2) * t_len + qi * tq
        n_full = jnp.minimum(((q0 // CHUNK + 1) * CHUNK) // tk, nkb)
        n_kv = jnp.minimum((((q0 + tq - 1) // CHUNK + 1) * CHUNK + tk - 1) // tk, nkb)
        m_ref[...] = jnp.full(m_ref.shape, NEG_INF, F32)
        l_ref[...] = jnp.zeros(l_ref.shape, F32)
        acc_ref[...] = jnp.zeros(acc_ref.shape, F32)

        def step(j, masked):
            ks = pl.ds(pl.multiple_of(j * tk, tk), tk)
            kr = kr_ref[ks, :]

            def scores(g):
                k = jnp.concatenate([k_ref[ks, g * LANES:(g + 1) * LANES], kr], axis=1)
                q = q_ref[rows, g * 2 * LANES:(g + 1) * 2 * LANES]
                s_ref[g] = lax.dot_general(q, k, (((1,), (1,)), ((), ())),
                                           preferred_element_type=F32)

            def softmax(g):
                for c in range(tq // rc):
                    rs = slice(c * rc, (c + 1) * rc)
                    s = s_ref[g, rs, :]
                    if masked:
                        qc = (q0 + c * rc + lax.broadcasted_iota(jnp.int32, (rc, 1), 0)) // CHUNK
                        kc = (j * tk + lax.broadcasted_iota(jnp.int32, (1, tk), 1)) // CHUNK
                        s = jnp.where(kc <= qc, s, NEG_INF)
                    m_prev = m_ref[g, rs, :]
                    m_next = jnp.maximum(m_prev, jnp.max(s, axis=1, keepdims=True))
                    alpha = jnp.exp2(m_prev - m_next)
                    p = jnp.exp2(s - _lane_tile(m_next, tk // LANES))
                    l_ref[g, rs, :] = alpha * l_ref[g, rs, :] + jnp.sum(p, axis=1, keepdims=True)
                    m_ref[g, rs, :] = m_next
                    a_ref[g, rs, :] = alpha
                    p_ref[g, rs, :] = p.astype(BF16)

            def weigh(g):
                acc_ref[g] = a_ref[g] * acc_ref[g] + jnp.dot(p_ref[g], v_ref[ks, g * LANES:(g + 1) * LANES],
                                                             preferred_element_type=F32)

            _staggered((scores, softmax, weigh), hg)

        _side_effect_loop(0, n_full, lambda j: step(j, False))
        _side_effect_loop(n_full, n_kv, lambda j: step(j, True))
        _emit_heads(o_ref, ssq_ref, rows, [acc_ref[g] / l_ref[g] for g in heads])

    _side_effect_loop(0, t_len // tq, q_block)


def _sb_kernel(q_ref, k_ref, v_ref, u_ref, *refs, hg, tq, tk, rc, q_off, t_len, s_len):
    o_ref, ssq_ref, z_ref, later_ref, hi_ref, lo_ref, r_ref, rsum_ref, acc_ref = refs[-9:]
    nkb = s_len // tk
    reps = tk // LANES
    heads = range(hg)

    def q_block(qi):
        rows = pl.ds(pl.multiple_of(qi * tq, tq), tq)
        q0 = q_off + pl.program_id(2) * t_len + qi * tq
        n_full = jnp.minimum(q0 // tk, nkb)
        n_kv = jnp.minimum((q0 + tq - 1 + tk - 1) // tk, nkb)
        r_ref[...] = jnp.zeros(r_ref.shape, F32)
        acc_ref[...] = jnp.zeros(acc_ref.shape, F32)

        def step(j, masked):
            ks = pl.ds(pl.multiple_of(j * tk, tk), tk)

            def before(c):
                qp = q0 + c * rc + lax.broadcasted_iota(jnp.int32, (rc, 1), 0)
                kp = j * tk + lax.broadcasted_iota(jnp.int32, (1, tk), 1)
                return kp < qp

            def logits(g):
                hs = slice(g * LANES, (g + 1) * LANES)
                z_ref[g] = lax.dot_general(q_ref[rows, hs], k_ref[ks, hs], (((1,), (1,)), ((), ())),
                                           preferred_element_type=F32)

            def log_fail(g):
                for c in range(tq // rc):
                    rs = slice(c * rc, (c + 1) * rc)
                    z = z_ref[g, rs, :]
                    fail = jnp.maximum(z, 0.0) + jnp.log(1.0 + jnp.exp2(-jnp.abs(z))) * LOG2E
                    z_ref[g, rs, :] = z - fail
                    if masked:
                        fail = jnp.where(before(c), fail, 0.0)
                    hi = fail.astype(BF16)
                    hi_ref[g, rs, :] = hi
                    lo_ref[g, rs, :] = (fail - hi.astype(F32)).astype(BF16)
                    rsum_ref[g, rs, :] = jnp.broadcast_to(jnp.sum(fail, axis=1, keepdims=True), (rc, LANES))
            u = u_ref[...]

            def suffix_sums(g):
                later_ref[g] = (jnp.dot(hi_ref[g], u, preferred_element_type=F32)
                                + jnp.dot(lo_ref[g], u, preferred_element_type=F32))

            def weights(g):
                for c in range(tq // rc):
                    rs = slice(c * rc, (c + 1) * rc)
                    r_prev = r_ref[g, rs, :]
                    w = jnp.exp2(z_ref[g, rs, :] - later_ref[g, rs, :] - _lane_tile(r_prev, reps))
                    if masked:
                        w = jnp.where(before(c), w, 0.0)
                    hi_ref[g, rs, :] = w.astype(BF16)
                    r_ref[g, rs, :] = r_prev + rsum_ref[g, rs, :]

            def weigh(g):
                acc_ref[g] += jnp.dot(hi_ref[g], v_ref[ks, g * LANES:(g + 1) * LANES],
                                      preferred_element_type=F32)

            for stage in (logits, log_fail, suffix_sums, weights, weigh):
                for g in heads:
                    stage(g)

        _side_effect_loop(0, n_kv - n_full, lambda t: step(n_kv - 1 - t, True))

        def more(c):
            return jnp.logical_and(c[0] >= 0, c[1] < -SB_LOG2_ZERO)

        def visit(c):
            step(c[0], False)
            return c[0] - 1, jnp.min(r_ref[...])

        lax.while_loop(more, visit, (n_full - 1, jnp.min(r_ref[...])))
        _emit_heads(o_ref, ssq_ref, rows, [acc_ref[g] for g in heads])

    _side_effect_loop(0, t_len // tq, q_block)


def _attention(kernel, q, kv_args, extra, *, batch, heads, hg, t_len, s_len, q_off, tq, tk, rc, scratch, name,
               parts, prev=None, half=0, span=1024):
    tq = min(tq, t_len)
    rc = min(rc, tq)
    span = min(span, t_len)
    nspan = t_len // span
    ngroups = heads // hg
    assert t_len % span == 0 and span % tq == 0 and tq % rc == 0 and s_len % tk == 0 and heads % hg == 0
    assert parts % ngroups == 0
    in_specs = [pl.BlockSpec((span, hg * (q.shape[-1] // heads)), lambda b, h, t: (b * nspan + t, h))]
    args = [q]
    for a, per_head in kv_args:
        if per_head:
            in_specs.append(pl.BlockSpec((s_len, hg * LANES), lambda b, h, t: (b, h)))
        else:
            in_specs.append(pl.BlockSpec((s_len, a.shape[-1]), lambda b, h, t: (b, 0)))
        args.append(a)
    for a in extra:
        in_specs.append(pl.BlockSpec(a.shape, lambda b, h, t, nd=a.ndim: (0,) * nd))
        args.append(a)
    aliases = {}
    if prev is not None:
        for out_idx, a in enumerate(prev):
            aliases[len(args)] = out_idx
            in_specs.append(pl.BlockSpec(memory_space=pl.ANY))
            args.append(a)
    m = batch * t_len
    return pl.pallas_call(
        functools.partial(kernel, hg=hg, tq=tq, tk=tk, rc=rc, q_off=q_off, t_len=span, s_len=s_len),
        grid=(batch, ngroups, nspan),
        in_specs=in_specs,
        out_specs=[pl.BlockSpec((span, hg * LANES), lambda b, h, t: (b * nspan + t, half * ngroups + h)),
                   pl.BlockSpec((None, parts // ngroups, span, LANES),
                                lambda b, h, t: (half, h, b * nspan + t, 0))],
        out_shape=[jax.ShapeDtypeStruct((m, 2 * heads * LANES), BF16),
                   jax.ShapeDtypeStruct((2, parts, m, LANES), F32)],
        scratch_shapes=scratch(hg, tq, tk),
        input_output_aliases=aliases,
        compiler_params=_params("parallel", "parallel", "parallel"),
        name=name,
    )(*args)


def mla_attention(q, k, kr, v, **kw):
    scratch = lambda hg, tq, tk: ([pltpu.VMEM((hg, tq, tk), F32), pltpu.VMEM((hg, tq, tk), BF16)]
                                  + [pltpu.VMEM((hg, tq, LANES), F32)] * 4)
    return _attention(_mla_kernel, q, [(k, True), (kr, False), (v, True)], [], scratch=scratch,
                      name="mla_attention", **kw)


def sb_attention(q, k, v, **kw):
    tk = kw["tk"]
    u = (lax.broadcasted_iota(jnp.int32, (tk, tk), 0)
         > lax.broadcasted_iota(jnp.int32, (tk, tk), 1)).astype(BF16)
    scratch = lambda hg, tq, tk: ([pltpu.VMEM((hg, tq, tk), F32)] * 2 + [pltpu.VMEM((hg, tq, tk), BF16)] * 2
                                  + [pltpu.VMEM((hg, tq, LANES), F32)] * 3)
    return _attention(_sb_kernel, q, [(k, True), (v, True)], [u], scratch=scratch,
                      name="sb_attention", **kw)


def _rope_table(pos, rope):
    half = rope // 2
    inv_freq = jnp.power(ROPE_THETA, -jnp.arange(half, dtype=F32) / half)
    ang = pos.astype(F32)[:, None] * inv_freq[None, :]
    cos, sin = jnp.cos(ang), jnp.sin(ang)
    return jnp.concatenate([cos, cos, -sin, sin], axis=1)


def _swap_halves(a, axis=-1):
    lo, hi = jnp.split(a, 2, axis=axis)
    return jnp.concatenate([hi, lo], axis=axis)


def _prep_layer(l, dims, w_in, w_q_b, w_kv_b, w_o, w_up, w_down, g_attn, g_q_nope, g_q_rope, g_k_rope,
                g_out_mla, g_out_sb, g_mlp):
    ql, kvl, rope, nope, vdim, heads, sbw = (dims[k] for k in
                                             ("ql", "kvl", "rope", "nope", "vdim", "heads", "sbw"))
    wt = jnp.swapaxes(w_in, 1, 2)[l]
    d = wt.shape[1]
    fold = lambda rows: (rows * g_attn[l][None, :]).astype(BF16)
    kr = wt[ql + kvl:ql + kvl + rope]
    lat_w = ql + kvl + 2 * rope
    pad = (-lat_w) % MXU_DIM
    w_lat = fold(jnp.concatenate([wt[:ql + kvl], kr, _swap_halves(kr, axis=0), jnp.zeros((pad, d), F32)], axis=0))
    o = ql + kvl + rope
    w_sq, w_sk, w_sv = (fold(wt[o + n * sbw:o + (n + 1) * sbw]) for n in range(3))

    wq = w_q_b[l].reshape(ql, heads, nope + rope)
    wq_r = wq[:, :, nope:]
    wq = jnp.concatenate([wq[:, :, :nope], wq_r, _swap_halves(wq_r)], axis=2).reshape(ql, heads * 2 * LANES)

    wkv = w_kv_b[l].reshape(kvl, heads, nope + vdim)
    wk = wkv[:, :, :nope].reshape(kvl, heads * nope)
    wv = wkv[:, :, nope:].reshape(kvl, heads * vdim)

    scale = LOG2E * (nope + rope) ** -0.5
    qrow = (jnp.concatenate([g_q_nope[l], g_q_rope[l], _swap_halves(g_q_rope[l])]) * scale).reshape(1, -1)
    krow = jnp.concatenate([g_k_rope[l], _swap_halves(g_k_rope[l])]).reshape(1, -1)
    bf = lambda a: a.astype(BF16)
    return dict(w_lat=w_lat, w_sq=w_sq, w_sk=w_sk, w_sv=w_sv, wq=bf(wq), wk=bf(wk), wv=bf(wv),
                w_o=cast_layer(w_o, l, gain=jnp.concatenate([g_out_mla[l], g_out_sb[l]])),
                w_up=cast_layer(w_up, l, gain=g_mlp[l]), w_down=cast_layer(w_down, l, tr=1024, tc=2048),
                qrow=qrow, krow=krow)


def _layer(x, past, tab, lw, gains, dims, leaves, layer, depth, *, batch, t_len, q_off, tq_mla,
           tk_mla, tq_sb, tk_sb, tm_lat):
    ql, kvl, rope, heads, sbh = (dims[k] for k in ("ql", "kvl", "rope", "heads", "sbh"))
    prev = leaves if leaves is not None else (None,) * 4
    slot = lambda n: Stack(prev[n], layer, depth)
    xb, x_ssq = cast_ssq(x)
    sb_scale = LOG2E * (dims["sbw"] // sbh) ** -0.5
    in_proj = functools.partial(matmul, xb, ssq=x_ssq, w_out_major=True)
    (sq,) = in_proj(lw["w_sq"], [BF16], out_scale=sb_scale)
    sk, skb = in_proj(lw["w_sk"], [F32, BF16], stack=slot(2))
    sv, svb = in_proj(lw["w_sv"], [F32, BF16], stack=slot(3))
    (lat,) = in_proj(lw["w_lat"], [F32])
    q_mla, ckv, ckvb, krope, kropeb = latent_post(
        lat, gains["g_q_a"].reshape(1, -1), lw["wq"], lw["qrow"], gains["g_kv_a"].reshape(1, -1),
        lw["krow"], tab, ql=ql, kvl=kvl, heads=heads, rope=rope, tm=tm_lat, stack_ckv=slot(0), stack_kr=slot(1))

    if past is None:
        s_len = t_len
        c_all, kr_all, sbk_all, sbv_all = ckvb, kropeb, skb, svb
    else:
        p_ckv, p_kr, p_sbk, p_sbv = past
        past_len = p_ckv.shape[1]
        s_len = -(-(past_len + t_len) // MXU_DIM) * MXU_DIM
        fill = s_len - past_len - t_len

        def rows(cached, new):
            width = new.shape[-1]
            return jnp.concatenate([cached.astype(BF16), new.reshape(batch, t_len, width),
                                    jnp.zeros((batch, fill, width), BF16)], axis=1).reshape(batch * s_len, width)

        c_all = rows(p_ckv, ckvb)
        kr_all = rows(jnp.pad(p_kr, ((0, 0), (0, 0), (0, LANES - rope))), kropeb)
        sbk_all = rows(p_sbk, skb)
        sbv_all = rows(p_sbv, svb)

    k_mla, v_mla = kv_expand(c_all, lw["wk"], lw["wv"], gains["g_k_nope"].reshape(1, -1), heads=heads)
    assert heads == sbh
    hg_mla, hg_sb = min(4, heads), min(4, sbh)
    common = dict(batch=batch, t_len=t_len, s_len=s_len, q_off=q_off, parts=heads // min(hg_mla, hg_sb))
    mixed = mla_attention(q_mla, k_mla, kr_all, v_mla, heads=heads, hg=hg_mla, tq=tq_mla, tk=tk_mla,
                          rc=32 * 8 * LANES // tk_mla, **common)
    merged, mix_ssq = sb_attention(sq, sbk_all, sbv_all, heads=sbh, hg=hg_sb, tq=tq_sb, tk=tk_sb, rc=128,
                                   prev=mixed, half=1, **common)

    h, hb, h_ssq = matmul(merged, lw["w_o"], [F32, BF16], res=x, ssq=mix_ssq, emit_ssq=True, tn=512)
    (u,) = matmul(hb, lw["w_up"], [BF16], act="relu2", ssq=h_ssq[None])
    (y,) = matmul(u, lw["w_down"], [F32], res=h, tk=4096)
    return y, (ckv, krope, sk, sv)


def kernel(x_prompt, x_sample, cache_mla_ckv, cache_mla_krope, cache_sb_k, cache_sb_v,
           g_attn, w_in, g_q_a, w_q_b, g_kv_a, w_kv_b, g_q_nope, g_q_rope, g_k_nope, g_k_rope,
           g_out_mla, g_out_sb, w_o, g_mlp, w_up, w_down):
    depth = w_in.shape[0]
    bp, tp, d = x_prompt.shape
    bs, ts, _ = x_sample.shape
    past_len = cache_mla_ckv.shape[2]
    sbh, sbd = cache_sb_k.shape[-2:]
    nope, rope = g_q_nope.shape[-1], g_q_rope.shape[-1]
    ql, kvl = g_q_a.shape[-1], g_kv_a.shape[-1]
    heads = w_q_b.shape[-1] // (nope + rope)
    vdim = w_kv_b.shape[-1] // heads - nope
    assert nope == LANES and vdim == LANES and sbd == LANES and 2 * rope == LANES
    dims = dict(ql=ql, kvl=kvl, rope=rope, nope=nope, vdim=vdim, heads=heads, sbh=sbh, sbw=sbh * sbd)

    tab_p = _rope_table(jnp.arange(tp, dtype=jnp.int32), rope)
    tab_s = _rope_table(past_len + jnp.arange(ts, dtype=jnp.int32), rope)

    hp = x_prompt.reshape(bp * tp, d)
    hs = x_sample.reshape(bs * ts, d)
    sb_k_rows, sb_v_rows = lax.optimization_barrier(
        (cache_sb_k.reshape(depth, bs, past_len, sbh * sbd), cache_sb_v.reshape(depth, bs, past_len, sbh * sbd)))
    rows_p = rows_s = None
    for l in range(depth):
        lw = _prep_layer(l, dims, w_in, w_q_b, w_kv_b, w_o, w_up, w_down, g_attn, g_q_nope, g_q_rope, g_k_rope,
                         g_out_mla, g_out_sb, g_mlp)
        gains = dict(g_q_a=g_q_a[l], g_kv_a=g_kv_a[l], g_k_nope=g_k_nope[l])
        hp, rows_p = _layer(hp, None, tab_p, lw, gains, dims, rows_p, l, depth, batch=bp, t_len=tp, q_off=0,
                            tq_mla=512, tk_mla=512, tq_sb=256, tk_sb=256, tm_lat=256)
        past = (cache_mla_ckv[l], cache_mla_krope[l], sb_k_rows[l], sb_v_rows[l])
        hs, rows_s = _layer(hs, past, tab_s, lw, gains, dims, rows_s, l, depth, batch=bs, t_len=ts,
                            q_off=past_len, tq_mla=ts, tk_mla=MXU_DIM, tq_sb=ts, tk_sb=MXU_DIM, tm_lat=ts)

    def leaves(rows, b, t):
        shapes = ((kvl,), (rope,), (sbh, sbd), (sbh, sbd))
        return tuple(r.reshape(depth, b, t, *s) for r, s in zip(rows, shapes))

    return (hp.reshape(bp, tp, d), hs.reshape(bs, ts, d)) + leaves(rows_p, bp, tp) + leaves(rows_s, bs, ts)
```

```python
import functools

import jax
import jax.numpy as jnp
from jax import lax
from jax.experimental import pallas as pl
from jax.experimental.pallas import tpu as pltpu

EPS = 1e-6
NEG_INF = -1e30
CHUNK = 64
ROPE_THETA = 10000.0
LANES = 128
MXU_DIM = 256
VMEM_LIMIT = 60 * 1024 * 1024
LOG2E = 1.4426950408889634
SB_LOG2_ZERO = -160.0

F32 = jnp.float32
BF16 = jnp.bfloat16


def _pick(n, cap, mult=LANES):
    if n <= cap:
        return n
    best = None
    for d in range(mult, cap + 1, mult):
        if n % d == 0:
            best = d
    assert best is not None, (n, cap, mult)
    return best


def _params(*sem):
    return pltpu.CompilerParams(dimension_semantics=sem, vmem_limit_bytes=VMEM_LIMIT)


def _rms(x, g):
    return x * lax.rsqrt(jnp.mean(x * x, axis=-1, keepdims=True) + EPS) * g


def _lane_tile(x, reps):
    return x if reps == 1 else jnp.concatenate([x] * reps, axis=1)


def _row_ssq(x):
    return jnp.broadcast_to(jnp.sum(x * x, axis=1, keepdims=True), (x.shape[0], LANES))


def _side_effect_loop(lo, hi, fn):
    lax.fori_loop(lo, hi, lambda j, c: (fn(j), c)[1], 0)


class Stack:
    def __init__(self, prev, layer, depth):
        self.prev, self.layer, self.depth = prev, layer, depth


def _cast_ssq_kernel(x_ref, o_ref, ssq_ref):
    x = x_ref[...]
    o_ref[...] = x.astype(o_ref.dtype)
    ssq_ref[...] = _row_ssq(x)


def cast_ssq(x, tm=256):
    m, d = x.shape
    tm = _pick(m, tm, 16)
    return pl.pallas_call(
        _cast_ssq_kernel,
        grid=(m // tm,),
        in_specs=[pl.BlockSpec((tm, d), lambda i: (i, 0))],
        out_specs=[pl.BlockSpec((tm, d), lambda i: (i, 0)),
                   pl.BlockSpec((None, None, tm, LANES), lambda i: (0, 0, i, 0))],
        out_shape=[jax.ShapeDtypeStruct((m, d), BF16), jax.ShapeDtypeStruct((1, 1, m, LANES), F32)],
        compiler_params=_params("parallel"),
        name="cast_ssq",
    )(x)


def _cast_kernel(x_ref, *refs):
    o_ref = refs[-1]
    x = x_ref[...]
    if len(refs) == 2:
        x = x * refs[0][...]
    o_ref[...] = x.astype(o_ref.dtype)


def cast_layer(w, layer, gain=None, tr=512, tc=4096):
    _, r, c = w.shape
    tr = _pick(r, tr, 16)
    tc = _pick(c, tc)
    in_specs = [pl.BlockSpec((None, tr, tc), lambda i, j: (layer, i, j))]
    args = [w]
    if gain is not None:
        in_specs.append(pl.BlockSpec((tr, 1), lambda i, j: (i, 0)))
        args.append(gain.reshape(r, 1))
    return pl.pallas_call(
        _cast_kernel,
        grid=(r // tr, c // tc),
        in_specs=in_specs,
        out_specs=pl.BlockSpec((tr, tc), lambda i, j: (i, j)),
        out_shape=jax.ShapeDtypeStruct((r, c), BF16),
        compiler_params=_params("parallel", "parallel"),
        name="cast_layer",
    )(*args)


def _mm_kernel(*refs, nk, act, groups, parts, out_scale, has_res, n_aliased, n_out, emit_ssq, acc_in_out,
               w_out_major):
    x_ref, w_ref = refs[0], refs[1]
    p = 2
    ssq_in_ref = res_ref = ssq_out_ref = None
    if groups:
        ssq_in_ref = refs[p]
        p += 1
    if has_res:
        res_ref = refs[p]
        p += 1
    p += n_aliased
    out_refs = refs[p:p + n_out]
    p += n_out
    if emit_ssq:
        ssq_out_ref = refs[p]
        p += 1

    def finish(r):
        if act == "relu2":
            r = jnp.maximum(r, 0.0)
            r = r * r
        if has_res:
            r = r + res_ref[...]
        for o_ref in out_refs:
            o_ref[...] = r.astype(o_ref.dtype).reshape(o_ref.shape)
        if emit_ssq:
            ssq_out_ref[...] = _row_ssq(r)

    if groups:
        kg = x_ref.shape[1] // groups
        part = None
        for g in range(groups):
            ssq = ssq_in_ref[g, 0]
            for n in range(1, parts):
                ssq = ssq + ssq_in_ref[g, n]
            rinv = lax.rsqrt(ssq * (1.0 / kg) + EPS) * out_scale
            ks = slice(g * kg, (g + 1) * kg)
            if w_out_major:
                term = lax.dot_general(x_ref[:, ks], w_ref[:, ks], (((1,), (1,)), ((), ())),
                                       preferred_element_type=F32)
            else:
                term = jnp.dot(x_ref[:, ks], w_ref[ks, :], preferred_element_type=F32)
            term = term * _lane_tile(rinv, term.shape[1] // LANES)
            part = term if part is None else part + term
    else:
        assert not w_out_major
        part = jnp.dot(x_ref[...], w_ref[...], preferred_element_type=F32)
    if nk == 1:
        finish(part)
        return
    acc_ref = out_refs[0] if acc_in_out else refs[p]
    k = pl.program_id(2)

    @pl.when(k == 0)
    def _():
        acc_ref[...] = part

    @pl.when(jnp.logical_and(k > 0, k < nk - 1))
    def _():
        acc_ref[...] += part

    @pl.when(k == nk - 1)
    def _():
        finish(acc_ref[...] + part)


def matmul(x, w, outs, act=None, res=None, ssq=None, out_scale=1.0, emit_ssq=False, stack=None,
           w_out_major=False, tm=1024, tn=1024, tk=None):
    m, kdim = x.shape
    n = w.shape[0] if w_out_major else w.shape[1]
    tm = _pick(m, tm, 8)
    tn = _pick(n, tn)
    tk = kdim if tk is None else _pick(kdim, tk)
    nk = kdim // tk
    grid = (m // tm, n // tn, nk)
    in_specs = [pl.BlockSpec((tm, tk), lambda i, j, k: (i, k)),
                pl.BlockSpec((tn, tk), lambda i, j, k: (j, k)) if w_out_major
                else pl.BlockSpec((tk, tn), lambda i, j, k: (k, j))]
    args = [x, w]
    groups = parts = 0
    if ssq is not None:
        groups, parts = ssq.shape[:2]
        assert nk == 1 and kdim % (groups * LANES) == 0
        in_specs.append(pl.BlockSpec((groups, parts, tm, LANES), lambda i, j, k: (0, 0, i, 0)))
        args.append(ssq)
    else:
        assert out_scale == 1.0
    if res is not None:
        in_specs.append(pl.BlockSpec((tm, tn), lambda i, j, k: (i, j)))
        args.append(res)
    aliases = {}
    if stack is not None and stack.prev is not None:
        aliases[len(args)] = 0
        in_specs.append(pl.BlockSpec(memory_space=pl.ANY))
        args.append(stack.prev)
    out_specs, out_shapes = [], []
    for o, dtype in enumerate(outs):
        if o == 0 and stack is not None:
            layer = stack.layer
            out_specs.append(pl.BlockSpec((None, tm, tn // LANES, LANES), lambda i, j, k: (layer, i, j, 0)))
            out_shapes.append(jax.ShapeDtypeStruct((stack.depth, m, n // LANES, LANES), dtype))
        else:
            out_specs.append(pl.BlockSpec((tm, tn), lambda i, j, k: (i, j)))
            out_shapes.append(jax.ShapeDtypeStruct((m, n), dtype))
    if emit_ssq:
        out_specs.append(pl.BlockSpec((None, tm, LANES), lambda i, j, k: (j, i, 0)))
        out_shapes.append(jax.ShapeDtypeStruct((n // tn, m, LANES), F32))
    acc_in_out = nk > 1 and act is None and stack is None and outs[0] == F32
    scratch = [pltpu.VMEM((tm, tn), F32)] if nk > 1 and not acc_in_out else []
    return pl.pallas_call(
        functools.partial(_mm_kernel, nk=nk, act=act, groups=groups, parts=parts, out_scale=out_scale,
                          w_out_major=w_out_major,
                          has_res=res is not None, n_aliased=len(aliases), n_out=len(outs), emit_ssq=emit_ssq,
                          acc_in_out=acc_in_out),
        grid=grid,
        in_specs=in_specs,
        out_specs=out_specs,
        out_shape=out_shapes,
        scratch_shapes=scratch,
        input_output_aliases=aliases,
        compiler_params=_params("parallel", "parallel", "arbitrary"),
        name="matmul",
    )(*args)


def _latent_kernel(lat_ref, gqa_ref, wq_ref, qrow_ref, gkva_ref, krow_ref, tab_ref, *refs,
                   ql, kvl, heads, rope):
    q_ref, ckv_ref, ckvb_ref, kr_ref, krb_ref = refs[-5:]
    lat = lat_ref[...]
    tab = tab_ref[...]

    def rotate(x, row):
        t = _rms(x, row) * tab
        return t + pltpu.roll(t, rope, axis=1)

    qn = _rms(lat[:, :ql], gqa_ref[...]).astype(BF16)
    q = jnp.dot(qn, wq_ref[...], preferred_element_type=F32)
    qrow = qrow_ref[...]
    for h in range(heads):
        base = h * 2 * LANES
        nope = _rms(q[:, base:base + LANES], qrow[:, :LANES])
        q_ref[:, base:base + LANES] = nope.astype(BF16)
        q_ref[:, base + LANES:base + 2 * LANES] = rotate(q[:, base + LANES:base + 2 * LANES],
                                                         qrow[:, LANES:]).astype(BF16)

    ckv = _rms(lat[:, ql:ql + kvl], gkva_ref[...])
    ckv_ref[...] = ckv
    ckvb_ref[...] = ckv.astype(BF16)

    kr = rotate(lat[:, ql + kvl:ql + kvl + LANES], krow_ref[...])
    kr_ref[...] = kr[:, :rope]
    lane = lax.broadcasted_iota(jnp.int32, kr.shape, 1)
    krb_ref[...] = jnp.where(lane < rope, kr, 0.0).astype(BF16)


def latent_post(lat, gqa, wq, qrow, gkva, krow, tab, *, ql, kvl, heads, rope, tm, stack_ckv, stack_kr):
    m = lat.shape[0]
    t = tab.shape[0]
    tm = min(tm, t)
    assert t % tm == 0 and m % tm == 0
    nt = t // tm
    layer, depth = stack_ckv.layer, stack_ckv.depth
    full = lambda a: pl.BlockSpec(a.shape, lambda i: (0,) * a.ndim)
    args = [lat, gqa, wq, qrow, gkva, krow, tab]
    in_specs = [pl.BlockSpec((tm, lat.shape[1]), lambda i: (i, 0)),
                full(gqa), full(wq), full(qrow), full(gkva), full(krow),
                pl.BlockSpec((tm, LANES), lambda i: (i % nt, 0))]
    aliases = {}
    for prev, out_idx in ((stack_ckv.prev, 1), (stack_kr.prev, 3)):
        if prev is not None:
            aliases[len(args)] = out_idx
            in_specs.append(pl.BlockSpec(memory_space=pl.ANY))
            args.append(prev)
    return pl.pallas_call(
        functools.partial(_latent_kernel, ql=ql, kvl=kvl, heads=heads, rope=rope),
        grid=(m // tm,),
        in_specs=in_specs,
        out_specs=[pl.BlockSpec((tm, heads * 2 * LANES), lambda i: (i, 0)),
                   pl.BlockSpec((None, tm, kvl), lambda i: (layer, i, 0)),
                   pl.BlockSpec((tm, kvl), lambda i: (i, 0)),
                   pl.BlockSpec((None, tm, rope), lambda i: (layer, i, 0)),
                   pl.BlockSpec((tm, LANES), lambda i: (i, 0))],
        out_shape=[jax.ShapeDtypeStruct((m, heads * 2 * LANES), BF16),
                   jax.ShapeDtypeStruct((depth, m, kvl), F32),
                   jax.ShapeDtypeStruct((m, kvl), BF16),
                   jax.ShapeDtypeStruct((depth, m, rope), F32),
                   jax.ShapeDtypeStruct((m, LANES), BF16)],
        input_output_aliases=aliases,
        compiler_params=_params("parallel"),
        name="latent_post",
    )(*args)


def _kv_expand_kernel(c_ref, wk_ref, wv_ref, g_ref, k_ref, v_ref, *, heads):
    c = c_ref[...]
    k = jnp.dot(c, wk_ref[...], preferred_element_type=F32)
    v = jnp.dot(c, wv_ref[...], preferred_element_type=F32)
    g = g_ref[...]
    for h in range(heads):
        sl = slice(h * LANES, (h + 1) * LANES)
        k_ref[:, sl] = _rms(k[:, sl], g).astype(BF16)
    v_ref[...] = v.astype(BF16)


def kv_expand(c, wk, wv, g, *, heads, tm=512):
    rows = c.shape[0]
    tm = _pick(rows, tm, 16)
    full = lambda a: pl.BlockSpec(a.shape, lambda i: (0,) * a.ndim)
    wide = jax.ShapeDtypeStruct((rows, heads * LANES), BF16)
    return pl.pallas_call(
        functools.partial(_kv_expand_kernel, heads=heads),
        grid=(rows // tm,),
        in_specs=[pl.BlockSpec((tm, c.shape[1]), lambda i: (i, 0)), full(wk), full(wv), full(g)],
        out_specs=[pl.BlockSpec((tm, heads * LANES), lambda i: (i, 0))] * 2,
        out_shape=[wide, wide],
        compiler_params=_params("parallel"),
        name="kv_expand",
    )(c, wk, wv, g)


def _emit_heads(o_ref, ssq_ref, rows, outs):
    ssq = None
    for g, o in enumerate(outs):
        o_ref[rows, g * LANES:(g + 1) * LANES] = o.astype(o_ref.dtype)
        ssq = _row_ssq(o) if ssq is None else ssq + _row_ssq(o)
    ssq_ref[0, rows, :] = ssq
    for n in range(1, ssq_ref.shape[0]):
        ssq_ref[n, rows, :] = jnp.zeros_like(ssq)


def _mla_kernel(q_ref, k_ref, kr_ref, v_ref, *refs, hg, tq, tk, rc, q_off, t_len, s_len):
    o_ref, ssq_ref, s_ref, p_ref, m_ref, l_ref, a_ref, acc_ref = refs[-8:]
    nkb = s_len // tk
    heads = range(hg)

    def q_block(qi):
        rows = pl.ds(pl.multiple_of(qi * tq, tq), tq)
        q0 = q_off + pl.program_id(2) * t_len + qi * tq
        n_full = jnp.minimum(((q0 // CHUNK + 1) * CHUNK) // tk, nkb)
        n_kv = jnp.minimum((((q0 + tq - 1) // CHUNK + 1) * CHUNK + tk - 1) // tk, nkb)
        m_ref[...] = jnp.full(m_ref.shape, NEG_INF, F32)
        l_ref[...] = jnp.zeros(l_ref.shape, F32)
        acc_ref[...] = jnp.zeros(acc_ref.shape, F32)

        def step(j, masked):
            ks = pl.ds(pl.multiple_of(j * tk, tk), tk)
            kr = kr_ref[ks, :]

            def scores(g):
                k = jnp.concatenate([k_ref[ks, g * LANES:(g + 1) * LANES], kr], axis=1)
                q = q_ref[rows, g * 2 * LANES:(g + 1) * 2 * LANES]
                s_ref[g] = lax.dot_general(q, k, (((1,), (1,)), ((), ())),
                                           preferred_element_type=F32)

            def softmax(g):
                for c in range(tq // rc):
                    rs = slice(c * rc, (c + 1) * rc)
                    s = s_ref[g, rs, :]
                    if masked:
                        qc = (q0 + c * rc + lax.broadcasted_iota(jnp.int32, (rc, 1), 0)) // CHUNK
                        kc = (j * tk + lax.broadcasted_iota(jnp.int32, (1, tk), 1)) // CHUNK
                        s = jnp.where(kc <= qc, s, NEG_INF)
                    m_prev = m_ref[g, rs, :]
                    m_next = jnp.maximum(m_prev, jnp.max(s, axis=1, keepdims=True))
                    alpha = jnp.exp2(m_prev - m_next)
                    p = jnp.exp2(s - _lane_tile(m_next, tk // LANES))
                    l_ref[g, rs, :] = alpha * l_ref[g, rs, :] + jnp.sum(p, axis=1, keepdims=True)
                    m_ref[g, rs, :] = m_next
                    a_ref[g, rs, :] = alpha
                    p_ref[g, rs, :] = p.astype(BF16)

            def weigh(g):
                acc_ref[g] = a_ref[g] * acc_ref[g] + jnp.dot(p_ref[g], v_ref[ks, g * LANES:(g + 1) * LANES],
                                                             preferred_element_type=F32)

            for stage in (scores, softmax, weigh):
                for g in heads:
                    stage(g)

        _side_effect_loop(0, n_full, lambda j: step(j, False))
        _side_effect_loop(n_full, n_kv, lambda j: step(j, True))
        _emit_heads(o_ref, ssq_ref, rows, [acc_ref[g] / l_ref[g] for g in heads])

    _side_effect_loop(0, t_len // tq, q_block)


def _sb_kernel(q_ref, k_ref, v_ref, u_ref, *refs, hg, tq, tk, rc, q_off, t_len, s_len):
    o_ref, ssq_ref, z_ref, later_ref, hi_ref, lo_ref, r_ref, rsum_ref, acc_ref = refs[-9:]
    nkb = s_len // tk
    reps = tk // LANES
    heads = range(hg)

    def q_block(qi):
        rows = pl.ds(pl.multiple_of(qi * tq, tq), tq)
        q0 = q_off + pl.program_id(2) * t_len + qi * tq
        n_full = jnp.minimum(q0 // tk, nkb)
        n_kv = jnp.minimum((q0 + tq - 1 + tk - 1) // tk, nkb)
        r_ref[...] = jnp.zeros(r_ref.shape, F32)
        acc_ref[...] = jnp.zeros(acc_ref.shape, F32)

        def step(j, masked):
            ks = pl.ds(pl.multiple_of(j * tk, tk), tk)

            def before(c):
                qp = q0 + c * rc + lax.broadcasted_iota(jnp.int32, (rc, 1), 0)
                kp = j * tk + lax.broadcasted_iota(jnp.int32, (1, tk), 1)
                return kp < qp

            def logits(g):
                hs = slice(g * LANES, (g + 1) * LANES)
                z_ref[g] = lax.dot_general(q_ref[rows, hs], k_ref[ks, hs], (((1,), (1,)), ((), ())),
                                           preferred_element_type=F32)

            def log_fail(g):
                for c in range(tq // rc):
                    rs = slice(c * rc, (c + 1) * rc)
                    z = z_ref[g, rs, :]
                    fail = jnp.maximum(z, 0.0) + jnp.log(1.0 + jnp.exp2(-jnp.abs(z))) * LOG2E
                    z_ref[g, rs, :] = z - fail
                    if masked:
                        fail = jnp.where(before(c), fail, 0.0)
                    hi = fail.astype(BF16)
                    hi_ref[g, rs, :] = hi
                    lo_ref[g, rs, :] = (fail - hi.astype(F32)).astype(BF16)
                    rsum_ref[g, rs, :] = jnp.broadcast_to(jnp.sum(fail, axis=1, keepdims=True), (rc, LANES))
            u = u_ref[...]

            def suffix_sums(g):
                later_ref[g] = (jnp.dot(hi_ref[g], u, preferred_element_type=F32)
                                + jnp.dot(lo_ref[g], u, preferred_element_type=F32))

            def weights(g):
                for c in range(tq // rc):
                    rs = slice(c * rc, (c + 1) * rc)
                    r_prev = r_ref[g, rs, :]
                    w = jnp.exp2(z_ref[g, rs, :] - later_ref[g, rs, :] - _lane_tile(r_prev, reps))
                    if masked:
                        w = jnp.where(before(c), w, 0.0)
                    hi_ref[g, rs, :] = w.astype(BF16)
                    r_ref[g, rs, :] = r_prev + rsum_ref[g, rs, :]

            def weigh(g):
                acc_ref[g] += jnp.dot(hi_ref[g], v_ref[ks, g * LANES:(g + 1) * LANES],
                                      preferred_element_type=F32)

            for stage in (logits, log_fail, suffix_sums, weights, weigh):
                for g in heads:
                    stage(g)

        _side_effect_loop(0, n_kv - n_full, lambda t: step(n_kv - 1 - t, True))

        def more(c):
            return jnp.logical_and(c[0] >= 0, c[1] < -SB_LOG2_ZERO)

        def visit(c):
            step(c[0], False)
            return c[0] - 1, jnp.min(r_ref[...])

        lax.while_loop(more, visit, (n_full - 1, jnp.min(r_ref[...])))
        _emit_heads(o_ref, ssq_ref, rows, [acc_ref[g] for g in heads])

    _side_effect_loop(0, t_len // tq, q_block)


def _attention(kernel, q, kv_args, extra, *, batch, heads, hg, t_len, s_len, q_off, tq, tk, rc, scratch, name,
               parts, prev=None, half=0, span=1024):
    tq = min(tq, t_len)
    rc = min(rc, tq)
    span = min(span, t_len)
    nspan = t_len // span
    ngroups = heads // hg
    assert t_len % span == 0 and span % tq == 0 and tq % rc == 0 and s_len % tk == 0 and heads % hg == 0
    assert parts % ngroups == 0
    in_specs = [pl.BlockSpec((span, hg * (q.shape[-1] // heads)), lambda b, h, t: (b * nspan + t, h))]
    args = [q]
    for a, per_head in kv_args:
        if per_head:
            in_specs.append(pl.BlockSpec((s_len, hg * LANES), lambda b, h, t: (b, h)))
        else:
            in_specs.append(pl.BlockSpec((s_len, a.shape[-1]), lambda b, h, t: (b, 0)))
        args.append(a)
    for a in extra:
        in_specs.append(pl.BlockSpec(a.shape, lambda b, h, t, nd=a.ndim: (0,) * nd))
        args.append(a)
    aliases = {}
    if prev is not None:
        for out_idx, a in enumerate(prev):
            aliases[len(args)] = out_idx
            in_specs.append(pl.BlockSpec(memory_space=pl.ANY))
            args.append(a)
    m = batch * t_len
    return pl.pallas_call(
        functools.partial(kernel, hg=hg, tq=tq, tk=tk, rc=rc, q_off=q_off, t_len=span, s_len=s_len),
        grid=(batch, ngroups, nspan),
        in_specs=in_specs,
        out_specs=[pl.BlockSpec((span, hg * LANES), lambda b, h, t: (b * nspan + t, half * ngroups + h)),
                   pl.BlockSpec((None, parts // ngroups, span, LANES),
                                lambda b, h, t: (half, h, b * nspan + t, 0))],
        out_shape=[jax.ShapeDtypeStruct((m, 2 * heads * LANES), BF16),
                   jax.ShapeDtypeStruct((2, parts, m, LANES), F32)],
        scratch_shapes=scratch(hg, tq, tk),
        input_output_aliases=aliases,
        compiler_params=_params("parallel", "parallel", "parallel"),
        name=name,
    )(*args)


def mla_attention(q, k, kr, v, **kw):
    scratch = lambda hg, tq, tk: ([pltpu.VMEM((hg, tq, tk), F32), pltpu.VMEM((hg, tq, tk), BF16)]
                                  + [pltpu.VMEM((hg, tq, LANES), F32)] * 4)
    return _attention(_mla_kernel, q, [(k, True), (kr, False), (v, True)], [], scratch=scratch,
                      name="mla_attention", **kw)


def sb_attention(q, k, v, **kw):
    tk = kw["tk"]
    u = (lax.broadcasted_iota(jnp.int32, (tk, tk), 0)
         > lax.broadcasted_iota(jnp.int32, (tk, tk), 1)).astype(BF16)
    scratch = lambda hg, tq, tk: ([pltpu.VMEM((hg, tq, tk), F32)] * 2 + [pltpu.VMEM((hg, tq, tk), BF16)] * 2
                                  + [pltpu.VMEM((hg, tq, LANES), F32)] * 3)
    return _attention(_sb_kernel, q, [(k, True), (v, True)], [u], scratch=scratch,
                      name="sb_attention", **kw)


def _rope_table(pos, rope):
    half = rope // 2
    inv_freq = jnp.power(ROPE_THETA, -jnp.arange(half, dtype=F32) / half)
    ang = pos.astype(F32)[:, None] * inv_freq[None, :]
    cos, sin = jnp.cos(ang), jnp.sin(ang)
    return jnp.concatenate([cos, cos, -sin, sin], axis=1)


def _swap_halves(a, axis=-1):
    lo, hi = jnp.split(a, 2, axis=axis)
    return jnp.concatenate([hi, lo], axis=axis)


def _prep_layer(l, dims, w_in, w_q_b, w_kv_b, w_o, w_up, w_down, g_attn, g_q_nope, g_q_rope, g_k_rope,
                g_out_mla, g_out_sb, g_mlp):
    ql, kvl, rope, nope, vdim, heads, sbw = (dims[k] for k in
                                             ("ql", "kvl", "rope", "nope", "vdim", "heads", "sbw"))
    wt = jnp.swapaxes(w_in, 1, 2)[l]
    d = wt.shape[1]
    fold = lambda rows: (rows * g_attn[l][None, :]).astype(BF16)
    kr = wt[ql + kvl:ql + kvl + rope]
    lat_w = ql + kvl + 2 * rope
    pad = (-lat_w) % MXU_DIM
    w_lat = fold(jnp.concatenate([wt[:ql + kvl], kr, _swap_halves(kr, axis=0), jnp.zeros((pad, d), F32)], axis=0))
    o = ql + kvl + rope
    w_sq, w_sk, w_sv = (fold(wt[o + n * sbw:o + (n + 1) * sbw]) for n in range(3))

    wq = w_q_b[l].reshape(ql, heads, nope + rope)
    wq_r = wq[:, :, nope:]
    wq = jnp.concatenate([wq[:, :, :nope], wq_r, _swap_halves(wq_r)], axis=2).reshape(ql, heads * 2 * LANES)

    wkv = w_kv_b[l].reshape(kvl, heads, nope + vdim)
    wk = wkv[:, :, :nope].reshape(kvl, heads * nope)
    wv = wkv[:, :, nope:].reshape(kvl, heads * vdim)

    scale = LOG2E * (nope + rope) ** -0.5
    qrow = (jnp.concatenate([g_q_nope[l], g_q_rope[l], _swap_halves(g_q_rope[l])]) * scale).reshape(1, -1)
    krow = jnp.concatenate([g_k_rope[l], _swap_halves(g_k_rope[l])]).reshape(1, -1)
    bf = lambda a: a.astype(BF16)
    return dict(w_lat=w_lat, w_sq=w_sq, w_sk=w_sk, w_sv=w_sv, wq=bf(wq), wk=bf(wk), wv=bf(wv),
                w_o=cast_layer(w_o, l, gain=jnp.concatenate([g_out_mla[l], g_out_sb[l]])),
                w_up=cast_layer(w_up, l, gain=g_mlp[l]), w_down=cast_layer(w_down, l),
                qrow=qrow, krow=krow)


def _layer(x, past, tab, lw, gains, dims, leaves, layer, depth, *, batch, t_len, q_off, tq_mla,
           tk_mla, tq_sb, tk_sb, tm_lat):
    ql, kvl, rope, heads, sbh = (dims[k] for k in ("ql", "kvl", "rope", "heads", "sbh"))
    prev = leaves if leaves is not None else (None,) * 4
    slot = lambda n: Stack(prev[n], layer, depth)
    xb, x_ssq = cast_ssq(x)
    sb_scale = LOG2E * (dims["sbw"] // sbh) ** -0.5
    in_proj = functools.partial(matmul, xb, ssq=x_ssq, w_out_major=True)
    (sq,) = in_proj(lw["w_sq"], [BF16], out_scale=sb_scale)
    sk, skb = in_proj(lw["w_sk"], [F32, BF16], stack=slot(2))
    sv, svb = in_proj(lw["w_sv"], [F32, BF16], stack=slot(3))
    (lat,) = in_proj(lw["w_lat"], [F32])
    q_mla, ckv, ckvb, krope, kropeb = latent_post(
        lat, gains["g_q_a"].reshape(1, -1), lw["wq"], lw["qrow"], gains["g_kv_a"].reshape(1, -1),
        lw["krow"], tab, ql=ql, kvl=kvl, heads=heads, rope=rope, tm=tm_lat, stack_ckv=slot(0), stack_kr=slot(1))

    if past is None:
        s_len = t_len
        c_all, kr_all, sbk_all, sbv_all = ckvb, kropeb, skb, svb
    else:
        p_ckv, p_kr, p_sbk, p_sbv = past
        past_len = p_ckv.shape[1]
        s_len = -(-(past_len + t_len) // MXU_DIM) * MXU_DIM
        fill = s_len - past_len - t_len

        def rows(cached, new):
            width = new.shape[-1]
            return jnp.concatenate([cached.astype(BF16), new.reshape(batch, t_len, width),
                                    jnp.zeros((batch, fill, width), BF16)], axis=1).reshape(batch * s_len, width)

        c_all = rows(p_ckv, ckvb)
        kr_all = rows(jnp.pad(p_kr, ((0, 0), (0, 0), (0, LANES - rope))), kropeb)
        sbk_all = rows(p_sbk, skb)
        sbv_all = rows(p_sbv, svb)

    k_mla, v_mla = kv_expand(c_all, lw["wk"], lw["wv"], gains["g_k_nope"].reshape(1, -1), heads=heads)
    assert heads == sbh
    hg_mla, hg_sb = min(4, heads), min(4, sbh)
    common = dict(batch=batch, t_len=t_len, s_len=s_len, q_off=q_off, parts=heads // min(hg_mla, hg_sb))
    mixed = mla_attention(q_mla, k_mla, kr_all, v_mla, heads=heads, hg=hg_mla, tq=tq_mla, tk=tk_mla,
                          rc=32 * 8 * LANES // tk_mla, **common)
    merged, mix_ssq = sb_attention(sq, sbk_all, sbv_all, heads=sbh, hg=hg_sb, tq=tq_sb, tk=tk_sb, rc=128,
                                   prev=mixed, half=1, **common)

    h, hb, h_ssq = matmul(merged, lw["w_o"], [F32, BF16], res=x, ssq=mix_ssq, emit_ssq=True, tn=512)
    (u,) = matmul(hb, lw["w_up"], [BF16], act="relu2", ssq=h_ssq[None])
    (y,) = matmul(u, lw["w_down"], [F32], res=h, tk=4096)
    return y, (ckv, krope, sk, sv)


def kernel(x_prompt, x_sample, cache_mla_ckv, cache_mla_krope, cache_sb_k, cache_sb_v,
           g_attn, w_in, g_q_a, w_q_b, g_kv_a, w_kv_b, g_q_nope, g_q_rope, g_k_nope, g_k_rope,
           g_out_mla, g_out_sb, w_o, g_mlp, w_up, w_down):
    depth = w_in.shape[0]
    bp, tp, d = x_prompt.shape
    bs, ts, _ = x_sample.shape
    past_len = cache_mla_ckv.shape[2]
    sbh, sbd = cache_sb_k.shape[-2:]
    nope, rope = g_q_nope.shape[-1], g_q_rope.shape[-1]
    ql, kvl = g_q_a.shape[-1], g_kv_a.shape[-1]
    heads = w_q_b.shape[-1] // (nope + rope)
    vdim = w_kv_b.shape[-1] // heads - nope
    assert nope == LANES and vdim == LANES and sbd == LANES and 2 * rope == LANES
    dims = dict(ql=ql, kvl=kvl, rope=rope, nope=nope, vdim=vdim, heads=heads, sbh=sbh, sbw=sbh * sbd)

    tab_p = _rope_table(jnp.arange(tp, dtype=jnp.int32), rope)
    tab_s = _rope_table(past_len + jnp.arange(ts, dtype=jnp.int32), rope)

    hp = x_prompt.reshape(bp * tp, d)
    hs = x_sample.reshape(bs * ts, d)
    sb_k_rows, sb_v_rows = lax.optimization_barrier(
        (cache_sb_k.reshape(depth, bs, past_len, sbh * sbd), cache_sb_v.reshape(depth, bs, past_len, sbh * sbd)))
    rows_p = rows_s = None
    for l in range(depth):
        lw = _prep_layer(l, dims, w_in, w_q_b, w_kv_b, w_o, w_up, w_down, g_attn, g_q_nope, g_q_rope, g_k_rope,
                         g_out_mla, g_out_sb, g_mlp)
        gains = dict(g_q_a=g_q_a[l], g_kv_a=g_kv_a[l], g_k_nope=g_k_nope[l])
        hp, rows_p = _layer(hp, None, tab_p, lw, gains, dims, rows_p, l, depth, batch=bp, t_len=tp, q_off=0,
                            tq_mla=512, tk_mla=512, tq_sb=256, tk_sb=256, tm_lat=256)
        past = (cache_mla_ckv[l], cache_mla_krope[l], sb_k_rows[l], sb_v_rows[l])
        hs, rows_s = _layer(hs, past, tab_s, lw, gains, dims, rows_s, l, depth, batch=bs, t_len=ts,
                            q_off=past_len, tq_mla=ts, tk_mla=MXU_DIM, tq_sb=ts, tk_sb=MXU_DIM, tm_lat=ts)

    def leaves(rows, b, t):
        shapes = ((kvl,), (rope,), (sbh, sbd), (sbh, sbd))
        return tuple(r.reshape(depth, b, t, *s) for r, s in zip(rows, shapes))

    return (hp.reshape(bp, tp, d), hs.reshape(bs, ts, d)) + leaves(rows_p, bp, tp) + leaves(rows_s, bs, ts)
```

```python
import functools

import jax
import jax.numpy as jnp
from jax import lax
from jax.experimental import pallas as pl
from jax.experimental.pallas import tpu as pltpu

EPS = 1e-6
NEG_INF = -1e30
CHUNK = 64
ROPE_THETA = 10000.0
LANES = 128
MXU_DIM = 256
VMEM_LIMIT = 60 * 1024 * 1024
LOG2E = 1.4426950408889634
SB_LOG2_ZERO = -160.0

F32 = jnp.float32
BF16 = jnp.bfloat16


def _pick(n, cap, mult=LANES):
    if n <= cap:
        return n
    best = None
    for d in range(mult, cap + 1, mult):
        if n % d == 0:
            best = d
    assert best is not None, (n, cap, mult)
    return best


def _params(*sem):
    return pltpu.CompilerParams(dimension_semantics=sem, vmem_limit_bytes=VMEM_LIMIT)


def _rms(x, g):
    return x * lax.rsqrt(jnp.mean(x * x, axis=-1, keepdims=True) + EPS) * g


def _lane_tile(x, reps):
    return x if reps == 1 else jnp.concatenate([x] * reps, axis=1)


def _row_ssq(x):
    return jnp.broadcast_to(jnp.sum(x * x, axis=1, keepdims=True), (x.shape[0], LANES))


def _side_effect_loop(lo, hi, fn):
    lax.fori_loop(lo, hi, lambda j, c: (fn(j), c)[1], 0)


class Stack:
    def __init__(self, prev, layer, depth):
        self.prev, self.layer, self.depth = prev, layer, depth


def _cast_ssq_kernel(x_ref, o_ref, ssq_ref):
    x = x_ref[...]
    o_ref[...] = x.astype(o_ref.dtype)
    ssq_ref[...] = _row_ssq(x)


def cast_ssq(x, tm=512):
    m, d = x.shape
    tm = _pick(m, tm, 16)
    return pl.pallas_call(
        _cast_ssq_kernel,
        grid=(m // tm,),
        in_specs=[pl.BlockSpec((tm, d), lambda i: (i, 0))],
        out_specs=[pl.BlockSpec((tm, d), lambda i: (i, 0)),
                   pl.BlockSpec((None, None, tm, LANES), lambda i: (0, 0, i, 0))],
        out_shape=[jax.ShapeDtypeStruct((m, d), BF16), jax.ShapeDtypeStruct((1, 1, m, LANES), F32)],
        compiler_params=_params("parallel"),
        name="cast_ssq",
    )(x)


def _cast_kernel(x_ref, *refs):
    o_ref = refs[-1]
    x = x_ref[...]
    if len(refs) == 2:
        x = x * refs[0][...]
    o_ref[...] = x.astype(o_ref.dtype)


def cast_layer(w, layer, gain=None, tr=512, tc=4096):
    _, r, c = w.shape
    tr = _pick(r, tr, 16)
    tc = _pick(c, tc)
    in_specs = [pl.BlockSpec((None, tr, tc), lambda i, j: (layer, i, j))]
    args = [w]
    if gain is not None:
        in_specs.append(pl.BlockSpec((tr, 1), lambda i, j: (i, 0)))
        args.append(gain.reshape(r, 1))
    return pl.pallas_call(
        _cast_kernel,
        grid=(r // tr, c // tc),
        in_specs=in_specs,
        out_specs=pl.BlockSpec((tr, tc), lambda i, j: (i, j)),
        out_shape=jax.ShapeDtypeStruct((r, c), BF16),
        compiler_params=_params("parallel", "parallel"),
        name="cast_layer",
    )(*args)


def _mm_kernel(*refs, nk, act, groups, parts, out_scale, has_res, n_aliased, n_out, emit_ssq, w_out_major):
    x_ref, w_ref = refs[0], refs[1]
    p = 2
    ssq_in_ref = res_ref = ssq_out_ref = None
    if groups:
        ssq_in_ref = refs[p]
        p += 1
    if has_res:
        res_ref = refs[p]
        p += 1
    p += n_aliased
    out_refs = refs[p:p + n_out]
    p += n_out
    if emit_ssq:
        ssq_out_ref = refs[p]
        p += 1

    def finish(r):
        if act == "relu2":
            r = jnp.maximum(r, 0.0)
            r = r * r
        if has_res:
            r = r + res_ref[...]
        for o_ref in out_refs:
            o_ref[...] = r.astype(o_ref.dtype).reshape(o_ref.shape)
        if emit_ssq:
            ssq_out_ref[...] = _row_ssq(r)

    if groups:
        kg = x_ref.shape[1] // groups
        part = None
        for g in range(groups):
            ssq = ssq_in_ref[g, 0]
            for n in range(1, parts):
                ssq = ssq + ssq_in_ref[g, n]
            rinv = lax.rsqrt(ssq * (1.0 / kg) + EPS) * out_scale
            ks = slice(g * kg, (g + 1) * kg)
            if w_out_major:
                term = lax.dot_general(x_ref[:, ks], w_ref[:, ks], (((1,), (1,)), ((), ())),
                                       preferred_element_type=F32)
            else:
                term = jnp.dot(x_ref[:, ks], w_ref[ks, :], preferred_element_type=F32)
            term = term * _lane_tile(rinv, term.shape[1] // LANES)
            part = term if part is None else part + term
    else:
        assert not w_out_major
        part = jnp.dot(x_ref[...], w_ref[...], preferred_element_type=F32)
    if nk == 1:
        finish(part)
        return
    acc_ref = out_refs[0]
    k = pl.program_id(2)

    @pl.when(k == 0)
    def _():
        acc_ref[...] = part

    @pl.when(jnp.logical_and(k > 0, k < nk - 1))
    def _():
        acc_ref[...] += part

    @pl.when(k == nk - 1)
    def _():
        finish(acc_ref[...] + part)


def matmul(x, w, outs, act=None, res=None, ssq=None, out_scale=1.0, emit_ssq=False, stack=None,
           w_out_major=False, tm=1024, tn=1024, tk=None):
    m, kdim = x.shape
    n = w.shape[0] if w_out_major else w.shape[1]
    tm = _pick(m, tm, 8)
    tn = _pick(n, tn)
    tk = kdim if tk is None else _pick(kdim, tk)
    nk = kdim // tk
    grid = (m // tm, n // tn, nk)
    in_specs = [pl.BlockSpec((tm, tk), lambda i, j, k: (i, k)),
                pl.BlockSpec((tn, tk), lambda i, j, k: (j, k)) if w_out_major
                else pl.BlockSpec((tk, tn), lambda i, j, k: (k, j))]
    args = [x, w]
    groups = parts = 0
    if ssq is not None:
        groups, parts = ssq.shape[:2]
        assert nk == 1 and kdim % (groups * LANES) == 0
        in_specs.append(pl.BlockSpec((groups, parts, tm, LANES), lambda i, j, k: (0, 0, i, 0)))
        args.append(ssq)
    else:
        assert out_scale == 1.0
    if res is not None:
        in_specs.append(pl.BlockSpec((tm, tn), lambda i, j, k: (i, j)))
        args.append(res)
    aliases = {}
    if stack is not None and stack.prev is not None:
        aliases[len(args)] = 0
        in_specs.append(pl.BlockSpec(memory_space=pl.ANY))
        args.append(stack.prev)
    out_specs, out_shapes = [], []
    for o, dtype in enumerate(outs):
        if o == 0 and stack is not None:
            layer = stack.layer
            out_specs.append(pl.BlockSpec((None, tm, tn // LANES, LANES), lambda i, j, k: (layer, i, j, 0)))
            out_shapes.append(jax.ShapeDtypeStruct((stack.depth, m, n // LANES, LANES), dtype))
        else:
            out_specs.append(pl.BlockSpec((tm, tn), lambda i, j, k: (i, j)))
            out_shapes.append(jax.ShapeDtypeStruct((m, n), dtype))
    if emit_ssq:
        out_specs.append(pl.BlockSpec((None, tm, LANES), lambda i, j, k: (j, i, 0)))
        out_shapes.append(jax.ShapeDtypeStruct((n // tn, m, LANES), F32))
    assert nk == 1 or (act is None and stack is None and outs[0] == F32)
    return pl.pallas_call(
        functools.partial(_mm_kernel, nk=nk, act=act, groups=groups, parts=parts, out_scale=out_scale,
                          w_out_major=w_out_major,
                          has_res=res is not None, n_aliased=len(aliases), n_out=len(outs), emit_ssq=emit_ssq),
        grid=grid,
        in_specs=in_specs,
        out_specs=out_specs,
        out_shape=out_shapes,
        input_output_aliases=aliases,
        compiler_params=_params("parallel", "parallel", "arbitrary"),
        name="matmul",
    )(*args)


def _latent_kernel(lat_ref, gqa_ref, wq_ref, qrow_ref, gkva_ref, krow_ref, tab_ref, *refs,
                   ql, kvl, heads, rope):
    q_ref, ckv_ref, ckvb_ref, kr_ref, krb_ref = refs[-5:]
    lat = lat_ref[...]
    tab = tab_ref[...]

    def rotate(x, row):
        t = _rms(x, row) * tab
        return t + pltpu.roll(t, rope, axis=1)

    qn = _rms(lat[:, :ql], gqa_ref[...]).astype(BF16)
    q = jnp.dot(qn, wq_ref[...], preferred_element_type=F32)
    qrow = qrow_ref[...]
    for h in range(heads):
        base = h * 2 * LANES
        nope = _rms(q[:, base:base + LANES], qrow[:, :LANES])
        q_ref[:, base:base + LANES] = nope.astype(BF16)
        q_ref[:, base + LANES:base + 2 * LANES] = rotate(q[:, base + LANES:base + 2 * LANES],
                                                         qrow[:, LANES:]).astype(BF16)

    ckv = _rms(lat[:, ql:ql + kvl], gkva_ref[...])
    ckv_ref[...] = ckv
    ckvb_ref[...] = ckv.astype(BF16)

    kr = rotate(lat[:, ql + kvl:ql + kvl + LANES], krow_ref[...])
    kr_ref[...] = kr[:, :rope]
    lane = lax.broadcasted_iota(jnp.int32, kr.shape, 1)
    krb_ref[...] = jnp.where(lane < rope, kr, 0.0).astype(BF16)


def latent_post(lat, gqa, wq, qrow, gkva, krow, tab, *, ql, kvl, heads, rope, tm, stack_ckv, stack_kr):
    m = lat.shape[0]
    t = tab.shape[0]
    tm = min(tm, t)
    assert t % tm == 0 and m % tm == 0
    nt = t // tm
    layer, depth = stack_ckv.layer, stack_ckv.depth
    full = lambda a: pl.BlockSpec(a.shape, lambda i: (0,) * a.ndim)
    args = [lat, gqa, wq, qrow, gkva, krow, tab]
    in_specs = [pl.BlockSpec((tm, lat.shape[1]), lambda i: (i, 0)),
                full(gqa), full(wq), full(qrow), full(gkva), full(krow),
                pl.BlockSpec((tm, LANES), lambda i: (i % nt, 0))]
    aliases = {}
    for prev, out_idx in ((stack_ckv.prev, 1), (stack_kr.prev, 3)):
        if prev is not None:
            aliases[len(args)] = out_idx
            in_specs.append(pl.BlockSpec(memory_space=pl.ANY))
            args.append(prev)
    return pl.pallas_call(
        functools.partial(_latent_kernel, ql=ql, kvl=kvl, heads=heads, rope=rope),
        grid=(m // tm,),
        in_specs=in_specs,
        out_specs=[pl.BlockSpec((tm, heads * 2 * LANES), lambda i: (i, 0)),
                   pl.BlockSpec((None, tm, kvl), lambda i: (layer, i, 0)),
                   pl.BlockSpec((tm, kvl), lambda i: (i, 0)),
                   pl.BlockSpec((None, tm, rope), lambda i: (layer, i, 0)),
                   pl.BlockSpec((tm, LANES), lambda i: (i, 0))],
        out_shape=[jax.ShapeDtypeStruct((m, heads * 2 * LANES), BF16),
                   jax.ShapeDtypeStruct((depth, m, kvl), F32),
                   jax.ShapeDtypeStruct((m, kvl), BF16),
                   jax.ShapeDtypeStruct((depth, m, rope), F32),
                   jax.ShapeDtypeStruct((m, LANES), BF16)],
        input_output_aliases=aliases,
        compiler_params=_params("parallel"),
        name="latent_post",
    )(*args)


def _kv_expand_kernel(c_ref, wk_ref, wv_ref, g_ref, k_ref, v_ref, *, heads):
    c = c_ref[...]
    k = jnp.dot(c, wk_ref[...], preferred_element_type=F32)
    v = jnp.dot(c, wv_ref[...], preferred_element_type=F32)
    g = g_ref[...]
    for h in range(heads):
        sl = slice(h * LANES, (h + 1) * LANES)
        k_ref[:, sl] = _rms(k[:, sl], g).astype(BF16)
    v_ref[...] = v.astype(BF16)


def kv_expand(c, wk, wv, g, *, heads, tm=512):
    rows = c.shape[0]
    tm = _pick(rows, tm, 16)
    full = lambda a: pl.BlockSpec(a.shape, lambda i: (0,) * a.ndim)
    wide = jax.ShapeDtypeStruct((rows, heads * LANES), BF16)
    return pl.pallas_call(
        functools.partial(_kv_expand_kernel, heads=heads),
        grid=(rows // tm,),
        in_specs=[pl.BlockSpec((tm, c.shape[1]), lambda i: (i, 0)), full(wk), full(wv), full(g)],
        out_specs=[pl.BlockSpec((tm, heads * LANES), lambda i: (i, 0))] * 2,
        out_shape=[wide, wide],
        compiler_params=_params("parallel"),
        name="kv_expand",
    )(c, wk, wv, g)


def _key_rows_kernel(c_ref, n_ref, o_ref, *, past_len, t_len):
    heads = c_ref.shape[1]
    o_ref[:past_len, :] = c_ref[...].reshape(past_len, heads * LANES).astype(o_ref.dtype)
    o_ref[past_len:past_len + t_len, :] = n_ref[...]
    fill = o_ref.shape[0] - past_len - t_len
    if fill:
        o_ref[past_len + t_len:, :] = jnp.zeros((fill, o_ref.shape[1]), o_ref.dtype)


def key_rows(cache, layer, new, *, t_len, s_len):
    _, batch, past_len, heads, lanes = cache.shape
    assert lanes == LANES and s_len >= past_len + t_len
    return pl.pallas_call(
        functools.partial(_key_rows_kernel, past_len=past_len, t_len=t_len),
        grid=(batch,),
        in_specs=[pl.BlockSpec((None, None, past_len, heads, LANES), lambda b: (layer, b, 0, 0, 0)),
                  pl.BlockSpec((t_len, heads * LANES), lambda b: (b, 0))],
        out_specs=pl.BlockSpec((s_len, heads * LANES), lambda b: (b, 0)),
        out_shape=jax.ShapeDtypeStruct((batch * s_len, heads * LANES), BF16),
        compiler_params=_params("parallel"),
        name="key_rows",
    )(cache, new)


def _emit_heads(o_ref, ssq_ref, rows, outs):
    ssq = None
    for g, o in enumerate(outs):
        o_ref[rows, g * LANES:(g + 1) * LANES] = o.astype(o_ref.dtype)
        ssq = _row_ssq(o) if ssq is None else ssq + _row_ssq(o)
    ssq_ref[0, rows, :] = ssq
    for n in range(1, ssq_ref.shape[0]):
        ssq_ref[n, rows, :] = jnp.zeros_like(ssq)


def _mla_kernel(q_ref, k_ref, kr_ref, v_ref, *refs, hg, tq, tk, rc, q_off, t_len, s_len):
    o_ref, ssq_ref, s_ref, p_ref, m_ref, l_ref, acc_ref = refs[-7:]
    nkb = s_len // tk
    heads = range(hg)

    def q_block(qi):
        rows = pl.ds(pl.multiple_of(qi * tq, tq), tq)
        q0 = q_off + pl.program_id(2) * t_len + qi * tq
        n_full = jnp.minimum(((q0 // CHUNK + 1) * CHUNK) // tk, nkb)
        n_kv = jnp.minimum((((q0 + tq - 1) // CHUNK + 1) * CHUNK + tk - 1) // tk, nkb)
        m_ref[...] = jnp.full(m_ref.shape, NEG_INF, F32)
        l_ref[...] = jnp.zeros(l_ref.shape, F32)
        acc_ref[...] = jnp.zeros(acc_ref.shape, F32)

        def step(j, masked):
            ks = pl.ds(pl.multiple_of(j * tk, tk), tk)
            kr = kr_ref[ks, :]

            def scores(g):
                k = jnp.concatenate([k_ref[ks, g * LANES:(g + 1) * LANES], kr], axis=1)
                q = q_ref[rows, g * 2 * LANES:(g + 1) * 2 * LANES]
                s_ref[g] = lax.dot_general(q, k, (((1,), (1,)), ((), ())),
                                           preferred_element_type=F32)

            def softmax(g):
                for c in range(tq // rc):
                    rs = slice(c * rc, (c + 1) * rc)
                    s = s_ref[g, rs, :]
                    if masked:
                        qc = (q0 + c * rc + lax.broadcasted_iota(jnp.int32, (rc, 1), 0)) // CHUNK
                        kc = (j * tk + lax.broadcasted_iota(jnp.int32, (1, tk), 1)) // CHUNK
                        s = jnp.where(kc <= qc, s, NEG_INF)
                    m_prev = m_ref[g, rs, :]
                    m_next = jnp.maximum(m_prev, jnp.max(s, axis=1, keepdims=True))
                    alpha = jnp.exp2(m_prev - m_next)
                    p = jnp.exp2(s - _lane_tile(m_next, tk // LANES))
                    l_ref[g, rs, :] = alpha * l_ref[g, rs, :] + jnp.sum(p, axis=1, keepdims=True)
                    m_ref[g, rs, :] = m_next
                    acc_ref[g, rs, :] = alpha * acc_ref[g, rs, :]
                    p_ref[g, rs, :] = p.astype(BF16)

            def weigh(g):
                acc_ref[g] += jnp.dot(p_ref[g], v_ref[ks, g * LANES:(g + 1) * LANES], preferred_element_type=F32)

            for stage in (scores, softmax, weigh):
                for g in heads:
                    stage(g)

        _side_effect_loop(0, n_full, lambda j: step(j, False))
        _side_effect_loop(n_full, n_kv, lambda j: step(j, True))
        _emit_heads(o_ref, ssq_ref, rows, [acc_ref[g] / l_ref[g] for g in heads])

    _side_effect_loop(0, t_len // tq, q_block)


def _sb_kernel(q_ref, k_ref, v_ref, u_ref, *refs, hg, tq, tk, rc, q_off, t_len, s_len):
    o_ref, ssq_ref, z_ref, later_ref, hi_ref, lo_ref, r_ref, rsum_ref, acc_ref = refs[-9:]
    nkb = s_len // tk
    reps = tk // LANES
    heads = range(hg)

    def q_block(qi):
        rows = pl.ds(pl.multiple_of(qi * tq, tq), tq)
        q0 = q_off + pl.program_id(2) * t_len + qi * tq
        n_full = jnp.minimum(q0 // tk, nkb)
        n_kv = jnp.minimum((q0 + tq - 1 + tk - 1) // tk, nkb)
        r_ref[...] = jnp.zeros(r_ref.shape, F32)
        acc_ref[...] = jnp.zeros(acc_ref.shape, F32)

        def step(j, masked):
            ks = pl.ds(pl.multiple_of(j * tk, tk), tk)

            def before(c):
                qp = q0 + c * rc + lax.broadcasted_iota(jnp.int32, (rc, 1), 0)
                kp = j * tk + lax.broadcasted_iota(jnp.int32, (1, tk), 1)
                return kp < qp

            def logits(g):
                hs = slice(g * LANES, (g + 1) * LANES)
                z_ref[g] = lax.dot_general(q_ref[rows, hs], k_ref[ks, hs], (((1,), (1,)), ((), ())),
                                           preferred_element_type=F32)

            def log_fail(g):
                for c in range(tq // rc):
                    rs = slice(c * rc, (c + 1) * rc)
                    z = z_ref[g, rs, :]
                    fail = jnp.maximum(z, 0.0) + jnp.log(1.0 + jnp.exp2(-jnp.abs(z))) * LOG2E
                    if masked:
                        fail = jnp.where(before(c), fail, 0.0)
                    hi = fail.astype(BF16)
                    hi_ref[g, rs, :] = hi
                    lo_ref[g, rs, :] = (fail - hi.astype(F32)).astype(BF16)
                    rsum_ref[g, rs, :] = jnp.broadcast_to(jnp.sum(fail, axis=1, keepdims=True), (rc, LANES))
            u = u_ref[...]

            def suffix_sums(g):
                later_ref[g] = (jnp.dot(hi_ref[g], u, preferred_element_type=F32)
                                + jnp.dot(lo_ref[g], u, preferred_element_type=F32))

            def weights(g):
                for c in range(tq // rc):
                    rs = slice(c * rc, (c + 1) * rc)
                    r_prev = r_ref[g, rs, :]
                    w = jnp.exp2(z_ref[g, rs, :] - later_ref[g, rs, :] - _lane_tile(r_prev, reps))
                    if masked:
                        w = jnp.where(before(c), w, 0.0)
                    hi_ref[g, rs, :] = w.astype(BF16)
                    r_ref[g, rs, :] = r_prev + rsum_ref[g, rs, :]

            def weigh(g):
                acc_ref[g] += jnp.dot(hi_ref[g], v_ref[ks, g * LANES:(g + 1) * LANES],
                                      preferred_element_type=F32)

            for stage in (logits, log_fail, suffix_sums, weights, weigh):
                for g in heads:
                    stage(g)

        _side_effect_loop(0, n_kv - n_full, lambda t: step(n_kv - 1 - t, True))

        def more(c):
            return jnp.logical_and(c[0] >= 0, c[1] < -SB_LOG2_ZERO)

        def visit(c):
            step(c[0], False)
            return c[0] - 1, jnp.min(r_ref[...])

        lax.while_loop(more, visit, (n_full - 1, jnp.min(r_ref[...])))
        _emit_heads(o_ref, ssq_ref, rows, [acc_ref[g] for g in heads])

    _side_effect_loop(0, t_len // tq, q_block)


def _attention(kernel, q, kv_args, extra, *, batch, heads, hg, t_len, s_len, q_off, tq, tk, rc, scratch, name,
               parts, prev=None, half=0, span=1024):
    tq = min(tq, t_len)
    rc = min(rc, tq)
    span = min(span, t_len)
    nspan = t_len // span
    ngroups = heads // hg
    assert t_len % span == 0 and span % tq == 0 and tq % rc == 0 and s_len % tk == 0 and heads % hg == 0
    assert parts % ngroups == 0
    in_specs = [pl.BlockSpec((span, hg * (q.shape[-1] // heads)), lambda b, h, t: (b * nspan + t, h))]
    args = [q]
    for a, per_head in kv_args:
        if per_head:
            in_specs.append(pl.BlockSpec((s_len, hg * LANES), lambda b, h, t: (b, h)))
        else:
            in_specs.append(pl.BlockSpec((s_len, a.shape[-1]), lambda b, h, t: (b, 0)))
        args.append(a)
    for a in extra:
        in_specs.append(pl.BlockSpec(a.shape, lambda b, h, t, nd=a.ndim: (0,) * nd))
        args.append(a)
    aliases = {}
    if prev is not None:
        for out_idx, a in enumerate(prev):
            aliases[len(args)] = out_idx
            in_specs.append(pl.BlockSpec(memory_space=pl.ANY))
            args.append(a)
    m = batch * t_len
    return pl.pallas_call(
        functools.partial(kernel, hg=hg, tq=tq, tk=tk, rc=rc, q_off=q_off, t_len=span, s_len=s_len),
        grid=(batch, ngroups, nspan),
        in_specs=in_specs,
        out_specs=[pl.BlockSpec((span, hg * LANES), lambda b, h, t: (b * nspan + t, half * ngroups + h)),
                   pl.BlockSpec((None, parts // ngroups, span, LANES),
                                lambda b, h, t: (half, h, b * nspan + t, 0))],
        out_shape=[jax.ShapeDtypeStruct((m, 2 * heads * LANES), BF16),
                   jax.ShapeDtypeStruct((2, parts, m, LANES), F32)],
        scratch_shapes=scratch(hg, tq, tk),
        input_output_aliases=aliases,
        compiler_params=_params("parallel", "parallel", "parallel"),
        name=name,
    )(*args)


def mla_attention(q, k, kr, v, **kw):
    scratch = lambda hg, tq, tk: ([pltpu.VMEM((hg, tq, tk), F32), pltpu.VMEM((hg, tq, tk), BF16)]
                                  + [pltpu.VMEM((hg, tq, LANES), F32)] * 3)
    return _attention(_mla_kernel, q, [(k, True), (kr, False), (v, True)], [], scratch=scratch,
                      name="mla_attention", **kw)


def sb_attention(q, k, v, **kw):
    tk = kw["tk"]
    u = (lax.broadcasted_iota(jnp.int32, (tk, tk), 0)
         >= lax.broadcasted_iota(jnp.int32, (tk, tk), 1)).astype(BF16)
    scratch = lambda hg, tq, tk: ([pltpu.VMEM((hg, tq, tk), F32)] * 2 + [pltpu.VMEM((hg, tq, tk), BF16)] * 2
                                  + [pltpu.VMEM((hg, tq, LANES), F32)] * 3)
    return _attention(_sb_kernel, q, [(k, True), (v, True)], [u], scratch=scratch,
                      name="sb_attention", **kw)


def _rope_table(pos, rope):
    half = rope // 2
    inv_freq = jnp.power(ROPE_THETA, -jnp.arange(half, dtype=F32) / half)
    ang = pos.astype(F32)[:, None] * inv_freq[None, :]
    cos, sin = jnp.cos(ang), jnp.sin(ang)
    return jnp.concatenate([cos, cos, -sin, sin], axis=1)


def _swap_halves(a, axis=-1):
    lo, hi = jnp.split(a, 2, axis=axis)
    return jnp.concatenate([hi, lo], axis=axis)


def _prep_layer(l, dims, w_in, w_q_b, w_kv_b, w_o, w_up, w_down, g_attn, g_q_nope, g_q_rope, g_k_rope,
                g_out_mla, g_out_sb, g_mlp):
    ql, kvl, rope, nope, vdim, heads, sbw = (dims[k] for k in
                                             ("ql", "kvl", "rope", "nope", "vdim", "heads", "sbw"))
    wt = jnp.swapaxes(w_in, 1, 2)[l]
    d = wt.shape[1]
    fold = lambda rows: (rows * g_attn[l][None, :]).astype(BF16)
    kr = wt[ql + kvl:ql + kvl + rope]
    lat_w = ql + kvl + 2 * rope
    pad = (-lat_w) % MXU_DIM
    w_lat = fold(jnp.concatenate([wt[:ql + kvl], kr, _swap_halves(kr, axis=0), jnp.zeros((pad, d), F32)], axis=0))
    o = ql + kvl + rope
    w_sq, w_sk, w_sv = (fold(wt[o + n * sbw:o + (n + 1) * sbw]) for n in range(3))

    wq = w_q_b[l].reshape(ql, heads, nope + rope)
    wq_r = wq[:, :, nope:]
    wq = jnp.concatenate([wq[:, :, :nope], wq_r, _swap_halves(wq_r)], axis=2).reshape(ql, heads * 2 * LANES)

    wkv = w_kv_b[l].reshape(kvl, heads, nope + vdim)
    wk = wkv[:, :, :nope].reshape(kvl, heads * nope)
    wv = wkv[:, :, nope:].reshape(kvl, heads * vdim)

    scale = LOG2E * (nope + rope) ** -0.5
    qrow = (jnp.concatenate([g_q_nope[l], g_q_rope[l], _swap_halves(g_q_rope[l])]) * scale).reshape(1, -1)
    krow = jnp.concatenate([g_k_rope[l], _swap_halves(g_k_rope[l])]).reshape(1, -1)
    bf = lambda a: a.astype(BF16)
    return dict(w_lat=w_lat, w_sq=w_sq, w_sk=w_sk, w_sv=w_sv, wq=bf(wq), wk=bf(wk), wv=bf(wv),
                w_o=cast_layer(w_o, l, gain=jnp.concatenate([g_out_mla[l], g_out_sb[l]])),
                w_up=cast_layer(w_up, l, gain=g_mlp[l]), w_down=cast_layer(w_down, l),
                qrow=qrow, krow=krow)


def _layer(x, past, tab, lw, gains, dims, leaves, layer, depth, *, batch, t_len, q_off, tq_mla,
           tk_mla, tq_sb, tk_sb, tm_lat):
    ql, kvl, rope, heads, sbh = (dims[k] for k in ("ql", "kvl", "rope", "heads", "sbh"))
    prev = leaves if leaves is not None else (None,) * 4
    slot = lambda n: Stack(prev[n], layer, depth)
    xb, x_ssq = cast_ssq(x)
    sb_scale = LOG2E * (dims["sbw"] // sbh) ** -0.5
    in_proj = functools.partial(matmul, xb, ssq=x_ssq, w_out_major=True)
    (sq,) = in_proj(lw["w_sq"], [BF16], out_scale=sb_scale)
    sk, skb = in_proj(lw["w_sk"], [F32, BF16], stack=slot(2))
    sv, svb = in_proj(lw["w_sv"], [F32, BF16], stack=slot(3))
    (lat,) = in_proj(lw["w_lat"], [F32])
    q_mla, ckv, ckvb, krope, kropeb = latent_post(
        lat, gains["g_q_a"].reshape(1, -1), lw["wq"], lw["qrow"], gains["g_kv_a"].reshape(1, -1),
        lw["krow"], tab, ql=ql, kvl=kvl, heads=heads, rope=rope, tm=tm_lat, stack_ckv=slot(0), stack_kr=slot(1))

    if past is None:
        s_len = t_len
        c_all, kr_all, sbk_all, sbv_all = ckvb, kropeb, skb, svb
    else:
        p_ckv, p_kr, p_sbk, p_sbv = past
        past_len = p_ckv.shape[1]
        s_len = -(-(past_len + t_len) // MXU_DIM) * MXU_DIM
        fill = s_len - past_len - t_len

        def rows(cached, new):
            width = new.shape[-1]
            return jnp.concatenate([cached.astype(BF16), new.reshape(batch, t_len, width),
                                    jnp.zeros((batch, fill, width), BF16)], axis=1).reshape(batch * s_len, width)

        c_all = rows(p_ckv, ckvb)
        kr_all = rows(jnp.pad(p_kr, ((0, 0), (0, 0), (0, LANES - rope))), kropeb)
        sbk_all = key_rows(p_sbk, layer, skb, t_len=t_len, s_len=s_len)
        sbv_all = key_rows(p_sbv, layer, svb, t_len=t_len, s_len=s_len)

    k_mla, v_mla = kv_expand(c_all, lw["wk"], lw["wv"], gains["g_k_nope"].reshape(1, -1), heads=heads)
    assert heads == sbh
    hg_mla, hg_sb = min(4, heads), min(4, sbh)
    common = dict(batch=batch, t_len=t_len, s_len=s_len, q_off=q_off, parts=heads // min(hg_mla, hg_sb))
    mixed = mla_attention(q_mla, k_mla, kr_all, v_mla, heads=heads, hg=hg_mla, tq=tq_mla, tk=tk_mla,
                          rc=32 * 8 * LANES // tk_mla, **common)
    merged, mix_ssq = sb_attention(sq, sbk_all, sbv_all, heads=sbh, hg=hg_sb, tq=tq_sb, tk=tk_sb, rc=128,
                                   prev=mixed, half=1, **common)

    h, hb, h_ssq = matmul(merged, lw["w_o"], [F32, BF16], res=x, ssq=mix_ssq, emit_ssq=True, tn=512)
    (u,) = matmul(hb, lw["w_up"], [BF16], act="relu2", ssq=h_ssq[None])
    (y,) = matmul(u, lw["w_down"], [F32], res=h, tk=4096)
    return y, (ckv, krope, sk, sv)


def kernel(x_prompt, x_sample, cache_mla_ckv, cache_mla_krope, cache_sb_k, cache_sb_v,
           g_attn, w_in, g_q_a, w_q_b, g_kv_a, w_kv_b, g_q_nope, g_q_rope, g_k_nope, g_k_rope,
           g_out_mla, g_out_sb, w_o, g_mlp, w_up, w_down):
    depth = w_in.shape[0]
    bp, tp, d = x_prompt.shape
    bs, ts, _ = x_sample.shape
    past_len = cache_mla_ckv.shape[2]
    sbh, sbd = cache_sb_k.shape[-2:]
    nope, rope = g_q_nope.shape[-1], g_q_rope.shape[-1]
    ql, kvl = g_q_a.shape[-1], g_kv_a.shape[-1]
    heads = w_q_b.shape[-1] // (nope + rope)
    vdim = w_kv_b.shape[-1] // heads - nope
    assert nope == LANES and vdim == LANES and sbd == LANES and 2 * rope == LANES
    dims = dict(ql=ql, kvl=kvl, rope=rope, nope=nope, vdim=vdim, heads=heads, sbh=sbh, sbw=sbh * sbd)

    tab_p = _rope_table(jnp.arange(tp, dtype=jnp.int32), rope)
    tab_s = _rope_table(past_len + jnp.arange(ts, dtype=jnp.int32), rope)

    hp = x_prompt.reshape(bp * tp, d)
    hs = x_sample.reshape(bs * ts, d)
    rows_p = rows_s = None
    for l in range(depth):
        lw = _prep_layer(l, dims, w_in, w_q_b, w_kv_b, w_o, w_up, w_down, g_attn, g_q_nope, g_q_rope, g_k_rope,
                         g_out_mla, g_out_sb, g_mlp)
        gains = dict(g_q_a=g_q_a[l], g_kv_a=g_kv_a[l], g_k_nope=g_k_nope[l])
        hp, rows_p = _layer(hp, None, tab_p, lw, gains, dims, rows_p, l, depth, batch=bp, t_len=tp, q_off=0,
                            tq_mla=512, tk_mla=512, tq_sb=256, tk_sb=256, tm_lat=256)
        past = (cache_mla_ckv[l], cache_mla_krope[l], cache_sb_k, cache_sb_v)
        hs, rows_s = _layer(hs, past, tab_s, lw, gains, dims, rows_s, l, depth, batch=bs, t_len=ts,
                            q_off=past_len, tq_mla=ts, tk_mla=MXU_DIM, tq_sb=ts, tk_sb=MXU_DIM, tm_lat=ts)

    def leaves(rows, b, t):
        shapes = ((kvl,), (rope,), (sbh, sbd), (sbh, sbd))
        return tuple(r.reshape(depth, b, t, *s) for r, s in zip(rows, shapes))

    return (hp.reshape(bp, tp, d), hs.reshape(bs, ts, d)) + leaves(rows_p, bp, tp) + leaves(rows_s, bs, ts)
```

```python
import functools

import jax
import jax.numpy as jnp
from jax import lax
from jax.experimental import pallas as pl
from jax.experimental.pallas import tpu as pltpu

EPS = 1e-6
NEG_INF = -1e30
CHUNK = 64
ROPE_THETA = 10000.0
LANES = 128
MXU_DIM = 256
VMEM_LIMIT = 60 * 1024 * 1024
LOG2E = 1.4426950408889634
SB_LOG2_ZERO = -160.0

F32 = jnp.float32
BF16 = jnp.bfloat16


def _pick(n, cap, mult=LANES):
    if n <= cap:
        return n
    best = None
    for d in range(mult, cap + 1, mult):
        if n % d == 0:
            best = d
    assert best is not None, (n, cap, mult)
    return best


def _params(*sem):
    return pltpu.CompilerParams(dimension_semantics=sem, vmem_limit_bytes=VMEM_LIMIT)


def _rms(x, g):
    return x * lax.rsqrt(jnp.mean(x * x, axis=-1, keepdims=True) + EPS) * g


def _lane_tile(x, reps):
    return x if reps == 1 else jnp.concatenate([x] * reps, axis=1)


def _row_ssq(x):
    return jnp.broadcast_to(jnp.sum(x * x, axis=1, keepdims=True), (x.shape[0], LANES))


def _side_effect_loop(lo, hi, fn):
    lax.fori_loop(lo, hi, lambda j, c: (fn(j), c)[1], 0)


class Stack:
    def __init__(self, prev, layer, depth):
        self.prev, self.layer, self.depth = prev, layer, depth


def _cast_ssq_kernel(x_ref, o_ref, ssq_ref):
    x = x_ref[...]
    o_ref[...] = x.astype(o_ref.dtype)
    ssq_ref[...] = _row_ssq(x)


def cast_ssq(x, tm=512):
    m, d = x.shape
    tm = _pick(m, tm, 16)
    return pl.pallas_call(
        _cast_ssq_kernel,
        grid=(m // tm,),
        in_specs=[pl.BlockSpec((tm, d), lambda i: (i, 0))],
        out_specs=[pl.BlockSpec((tm, d), lambda i: (i, 0)),
                   pl.BlockSpec((None, None, tm, LANES), lambda i: (0, 0, i, 0))],
        out_shape=[jax.ShapeDtypeStruct((m, d), BF16), jax.ShapeDtypeStruct((1, 1, m, LANES), F32)],
        compiler_params=_params("parallel"),
        name="cast_ssq",
    )(x)


def _cast_kernel(x_ref, *refs):
    o_ref = refs[-1]
    x = x_ref[...]
    if len(refs) == 2:
        x = x * refs[0][...]
    o_ref[...] = x.astype(o_ref.dtype)


def cast_layer(w, layer, gain=None, gain_axis=0, tr=512, tc=4096):
    _, r, c = w.shape
    tr = _pick(r, tr, 16)
    tc = _pick(c, tc)
    in_specs = [pl.BlockSpec((None, tr, tc), lambda i, j: (layer, i, j))]
    args = [w]
    if gain is not None and gain_axis == 0:
        in_specs.append(pl.BlockSpec((tr, 1), lambda i, j: (i, 0)))
        args.append(gain.reshape(r, 1))
    elif gain is not None:
        in_specs.append(pl.BlockSpec((1, tc), lambda i, j: (0, j)))
        args.append(gain.reshape(1, c))
    return pl.pallas_call(
        _cast_kernel,
        grid=(r // tr, c // tc),
        in_specs=in_specs,
        out_specs=pl.BlockSpec((tr, tc), lambda i, j: (i, j)),
        out_shape=jax.ShapeDtypeStruct((r, c), BF16),
        compiler_params=_params("parallel", "parallel"),
        name="cast_layer",
    )(*args)


def _mm_kernel(*refs, nk, act, groups, parts, out_scale, has_res, n_aliased, n_out, emit_ssq, w_out_major):
    x_ref, w_ref = refs[0], refs[1]
    p = 2
    ssq_in_ref = res_ref = ssq_out_ref = None
    if groups:
        ssq_in_ref = refs[p]
        p += 1
    if has_res:
        res_ref = refs[p]
        p += 1
    p += n_aliased
    out_refs = refs[p:p + n_out]
    p += n_out
    if emit_ssq:
        ssq_out_ref = refs[p]
        p += 1

    def finish(r):
        if act == "relu2":
            r = jnp.maximum(r, 0.0)
            r = r * r
        if has_res:
            r = r + res_ref[...]
        for o_ref in out_refs:
            o_ref[...] = r.astype(o_ref.dtype).reshape(o_ref.shape)
        if emit_ssq:
            ssq_out_ref[...] = _row_ssq(r)

    if groups:
        kg = x_ref.shape[1] // groups
        part = None
        for g in range(groups):
            ssq = ssq_in_ref[g, 0]
            for n in range(1, parts):
                ssq = ssq + ssq_in_ref[g, n]
            rinv = lax.rsqrt(ssq * (1.0 / kg) + EPS) * out_scale
            ks = slice(g * kg, (g + 1) * kg)
            if w_out_major:
                term = lax.dot_general(x_ref[:, ks], w_ref[:, ks], (((1,), (1,)), ((), ())),
                                       preferred_element_type=F32)
            else:
                term = jnp.dot(x_ref[:, ks], w_ref[ks, :], preferred_element_type=F32)
            term = term * _lane_tile(rinv, term.shape[1] // LANES)
            part = term if part is None else part + term
    else:
        assert not w_out_major
        part = jnp.dot(x_ref[...], w_ref[...], preferred_element_type=F32)
    if nk == 1:
        finish(part)
        return
    acc_ref = out_refs[0]
    k = pl.program_id(2)

    @pl.when(k == 0)
    def _():
        acc_ref[...] = part

    @pl.when(jnp.logical_and(k > 0, k < nk - 1))
    def _():
        acc_ref[...] += part

    @pl.when(k == nk - 1)
    def _():
        finish(acc_ref[...] + part)


def matmul(x, w, outs, act=None, res=None, ssq=None, out_scale=1.0, emit_ssq=False, stack=None,
           w_out_major=False, w_rows=None, tm=1024, tn=1024, tk=None):
    m, kdim = x.shape
    row0, n = w_rows if w_rows is not None else (0, w.shape[0] if w_out_major else w.shape[1])
    assert w_rows is None or w_out_major
    tm = _pick(m, tm, 8)
    tn = _pick(n, tn)
    tk = kdim if tk is None else _pick(kdim, tk)
    nk = kdim // tk
    grid = (m // tm, n // tn, nk)
    if w_rows is not None:
        assert row0 % 16 == 0 and tn % 16 == 0
        w_spec = pl.BlockSpec((pl.Element(tn), pl.Element(tk)),
                              lambda i, j, k: (pl.multiple_of(row0 + j * tn, 16), pl.multiple_of(k * tk, LANES)))
    elif w_out_major:
        w_spec = pl.BlockSpec((tn, tk), lambda i, j, k: (j, k))
    else:
        w_spec = pl.BlockSpec((tk, tn), lambda i, j, k: (k, j))
    in_specs = [pl.BlockSpec((tm, tk), lambda i, j, k: (i, k)), w_spec]
    args = [x, w]
    groups = parts = 0
    if ssq is not None:
        groups, parts = ssq.shape[:2]
        assert nk == 1 and kdim % (groups * LANES) == 0
        in_specs.append(pl.BlockSpec((groups, parts, tm, LANES), lambda i, j, k: (0, 0, i, 0)))
        args.append(ssq)
    else:
        assert out_scale == 1.0
    if res is not None:
        in_specs.append(pl.BlockSpec((tm, tn), lambda i, j, k: (i, j)))
        args.append(res)
    aliases = {}
    if stack is not None and stack.prev is not None:
        aliases[len(args)] = 0
        in_specs.append(pl.BlockSpec(memory_space=pl.ANY))
        args.append(stack.prev)
    out_specs, out_shapes = [], []
    for o, dtype in enumerate(outs):
        if o == 0 and stack is not None:
            layer = stack.layer
            out_specs.append(pl.BlockSpec((None, tm, tn // LANES, LANES), lambda i, j, k: (layer, i, j, 0)))
            out_shapes.append(jax.ShapeDtypeStruct((stack.depth, m, n // LANES, LANES), dtype))
        else:
            out_specs.append(pl.BlockSpec((tm, tn), lambda i, j, k: (i, j)))
            out_shapes.append(jax.ShapeDtypeStruct((m, n), dtype))
    if emit_ssq:
        out_specs.append(pl.BlockSpec((None, tm, LANES), lambda i, j, k: (j, i, 0)))
        out_shapes.append(jax.ShapeDtypeStruct((n // tn, m, LANES), F32))
    assert nk == 1 or (act is None and stack is None and outs[0] == F32)
    return pl.pallas_call(
        functools.partial(_mm_kernel, nk=nk, act=act, groups=groups, parts=parts, out_scale=out_scale,
                          w_out_major=w_out_major,
                          has_res=res is not None, n_aliased=len(aliases), n_out=len(outs), emit_ssq=emit_ssq),
        grid=grid,
        in_specs=in_specs,
        out_specs=out_specs,
        out_shape=out_shapes,
        input_output_aliases=aliases,
        compiler_params=_params("parallel", "parallel", "arbitrary"),
        name="matmul",
    )(*args)


def _latent_kernel(lat_ref, gqa_ref, wq_ref, qrow_ref, gkva_ref, krow_ref, tab_ref, *refs,
                   ql, kvl, heads, rope):
    q_ref, ckv_ref, ckvb_ref, kr_ref, krb_ref = refs[-5:]
    lat = lat_ref[...]
    tab = tab_ref[...]

    def rotate(x, row):
        t = _rms(x, row) * tab
        return t + pltpu.roll(t, rope, axis=1)

    qn = _rms(lat[:, :ql], gqa_ref[...]).astype(BF16)
    q = jnp.dot(qn, wq_ref[...], preferred_element_type=F32)
    qrow = qrow_ref[...]
    for h in range(heads):
        base = h * 2 * LANES
        nope = _rms(q[:, base:base + LANES], qrow[:, :LANES])
        q_ref[:, base:base + LANES] = nope.astype(BF16)
        q_ref[:, base + LANES:base + 2 * LANES] = rotate(q[:, base + LANES:base + 2 * LANES],
                                                         qrow[:, LANES:]).astype(BF16)

    ckv = _rms(lat[:, ql:ql + kvl], gkva_ref[...])
    ckv_ref[...] = ckv
    ckvb_ref[...] = ckv.astype(BF16)

    kr = rotate(lat[:, ql + kvl:ql + kvl + LANES], krow_ref[...])
    kr_ref[...] = kr[:, :rope]
    lane = lax.broadcasted_iota(jnp.int32, kr.shape, 1)
    krb_ref[...] = jnp.where(lane < rope, kr, 0.0).astype(BF16)


def latent_post(lat, gqa, wq, qrow, gkva, krow, tab, *, ql, kvl, heads, rope, tm, stack_ckv, stack_kr):
    m = lat.shape[0]
    t = tab.shape[0]
    tm = min(tm, t)
    assert t % tm == 0 and m % tm == 0
    nt = t // tm
    layer, depth = stack_ckv.layer, stack_ckv.depth
    full = lambda a: pl.BlockSpec(a.shape, lambda i: (0,) * a.ndim)
    args = [lat, gqa, wq, qrow, gkva, krow, tab]
    in_specs = [pl.BlockSpec((tm, lat.shape[1]), lambda i: (i, 0)),
                full(gqa), full(wq), full(qrow), full(gkva), full(krow),
                pl.BlockSpec((tm, LANES), lambda i: (i % nt, 0))]
    aliases = {}
    for prev, out_idx in ((stack_ckv.prev, 1), (stack_kr.prev, 3)):
        if prev is not None:
            aliases[len(args)] = out_idx
            in_specs.append(pl.BlockSpec(memory_space=pl.ANY))
            args.append(prev)
    return pl.pallas_call(
        functools.partial(_latent_kernel, ql=ql, kvl=kvl, heads=heads, rope=rope),
        grid=(m // tm,),
        in_specs=in_specs,
        out_specs=[pl.BlockSpec((tm, heads * 2 * LANES), lambda i: (i, 0)),
                   pl.BlockSpec((None, tm, kvl), lambda i: (layer, i, 0)),
                   pl.BlockSpec((tm, kvl), lambda i: (i, 0)),
                   pl.BlockSpec((None, tm, rope), lambda i: (layer, i, 0)),
                   pl.BlockSpec((tm, LANES), lambda i: (i, 0))],
        out_shape=[jax.ShapeDtypeStruct((m, heads * 2 * LANES), BF16),
                   jax.ShapeDtypeStruct((depth, m, kvl), F32),
                   jax.ShapeDtypeStruct((m, kvl), BF16),
                   jax.ShapeDtypeStruct((depth, m, rope), F32),
                   jax.ShapeDtypeStruct((m, LANES), BF16)],
        input_output_aliases=aliases,
        compiler_params=_params("parallel"),
        name="latent_post",
    )(*args)


def _kv_expand_kernel(c_ref, wk_ref, wv_ref, g_ref, k_ref, v_ref, *, heads):
    c = c_ref[...]
    k = jnp.dot(c, wk_ref[...], preferred_element_type=F32)
    v = jnp.dot(c, wv_ref[...], preferred_element_type=F32)
    g = g_ref[...]
    for h in range(heads):
        sl = slice(h * LANES, (h + 1) * LANES)
        k_ref[:, sl] = _rms(k[:, sl], g).astype(BF16)
    v_ref[...] = v.astype(BF16)


def kv_expand(c, wk, wv, g, *, heads, tm=512):
    rows = c.shape[0]
    tm = _pick(rows, tm, 16)
    full = lambda a: pl.BlockSpec(a.shape, lambda i: (0,) * a.ndim)
    wide = jax.ShapeDtypeStruct((rows, heads * LANES), BF16)
    return pl.pallas_call(
        functools.partial(_kv_expand_kernel, heads=heads),
        grid=(rows // tm,),
        in_specs=[pl.BlockSpec((tm, c.shape[1]), lambda i: (i, 0)), full(wk), full(wv), full(g)],
        out_specs=[pl.BlockSpec((tm, heads * LANES), lambda i: (i, 0))] * 2,
        out_shape=[wide, wide],
        compiler_params=_params("parallel"),
        name="kv_expand",
    )(c, wk, wv, g)


def _key_rows_kernel(c_ref, n_ref, o_ref, *, past_len, t_len):
    heads = c_ref.shape[1]
    o_ref[:past_len, :] = c_ref[...].reshape(past_len, heads * LANES).astype(o_ref.dtype)
    o_ref[past_len:past_len + t_len, :] = n_ref[...]
    fill = o_ref.shape[0] - past_len - t_len
    if fill:
        o_ref[past_len + t_len:, :] = jnp.zeros((fill, o_ref.shape[1]), o_ref.dtype)


def key_rows(cache, layer, new, *, t_len, s_len):
    _, batch, past_len, heads, lanes = cache.shape
    assert lanes == LANES and s_len >= past_len + t_len
    return pl.pallas_call(
        functools.partial(_key_rows_kernel, past_len=past_len, t_len=t_len),
        grid=(batch,),
        in_specs=[pl.BlockSpec((None, None, past_len, heads, LANES), lambda b: (layer, b, 0, 0, 0)),
                  pl.BlockSpec((t_len, heads * LANES), lambda b: (b, 0))],
        out_specs=pl.BlockSpec((s_len, heads * LANES), lambda b: (b, 0)),
        out_shape=jax.ShapeDtypeStruct((batch * s_len, heads * LANES), BF16),
        compiler_params=_params("parallel"),
        name="key_rows",
    )(cache, new)


def _emit_heads(o_ref, ssq_ref, rows, outs):
    ssq = None
    for g, o in enumerate(outs):
        o_ref[rows, g * LANES:(g + 1) * LANES] = o.astype(o_ref.dtype)
        ssq = _row_ssq(o) if ssq is None else ssq + _row_ssq(o)
    ssq_ref[0, rows, :] = ssq
    for n in range(1, ssq_ref.shape[0]):
        ssq_ref[n, rows, :] = jnp.zeros_like(ssq)


def _mla_kernel(q_ref, k_ref, kr_ref, v_ref, *refs, hg, tq, tk, rc, q_off, t_len, s_len):
    o_ref, ssq_ref, s_ref, p_ref, m_ref, l_ref, acc_ref = refs[-7:]
    nkb = s_len // tk
    heads = range(hg)

    def q_block(qi):
        rows = pl.ds(pl.multiple_of(qi * tq, tq), tq)
        q0 = q_off + pl.program_id(2) * t_len + qi * tq
        n_full = jnp.minimum(((q0 // CHUNK + 1) * CHUNK) // tk, nkb)
        n_kv = jnp.minimum((((q0 + tq - 1) // CHUNK + 1) * CHUNK + tk - 1) // tk, nkb)
        m_ref[...] = jnp.full(m_ref.shape, NEG_INF, F32)
        l_ref[...] = jnp.zeros(l_ref.shape, F32)
        acc_ref[...] = jnp.zeros(acc_ref.shape, F32)

        def step(j, masked):
            ks = pl.ds(pl.multiple_of(j * tk, tk), tk)
            kr = kr_ref[ks, :]

            def scores(g):
                k = jnp.concatenate([k_ref[ks, g * LANES:(g + 1) * LANES], kr], axis=1)
                q = q_ref[rows, g * 2 * LANES:(g + 1) * 2 * LANES]
                s_ref[g] = lax.dot_general(q, k, (((1,), (1,)), ((), ())),
                                           preferred_element_type=F32)

            def softmax(g):
                for c in range(tq // rc):
                    rs = slice(c * rc, (c + 1) * rc)
                    s = s_ref[g, rs, :]
                    if masked:
                        qc = (q0 + c * rc + lax.broadcasted_iota(jnp.int32, (rc, 1), 0)) // CHUNK
                        kc = (j * tk + lax.broadcasted_iota(jnp.int32, (1, tk), 1)) // CHUNK
                        s = jnp.where(kc <= qc, s, NEG_INF)
                    m_prev = m_ref[g, rs, :]
                    m_next = jnp.maximum(m_prev, jnp.max(s, axis=1, keepdims=True))
                    alpha = jnp.exp2(m_prev - m_next)
                    p = jnp.exp2(s - _lane_tile(m_next, tk // LANES))
                    l_ref[g, rs, :] = alpha * l_ref[g, rs, :] + jnp.sum(p, axis=1, keepdims=True)
                    m_ref[g, rs, :] = m_next
                    acc_ref[g, rs, :] = alpha * acc_ref[g, rs, :]
                    p_ref[g, rs, :] = p.astype(BF16)

            def weigh(g):
                acc_ref[g] += jnp.dot(p_ref[g], v_ref[ks, g * LANES:(g + 1) * LANES], preferred_element_type=F32)

            for stage in (scores, softmax, weigh):
                for g in heads:
                    stage(g)

        _side_effect_loop(0, n_full, lambda j: step(j, False))
        _side_effect_loop(n_full, n_kv, lambda j: step(j, True))
        _emit_heads(o_ref, ssq_ref, rows, [acc_ref[g] / l_ref[g] for g in heads])

    _side_effect_loop(0, t_len // tq, q_block)


def _sb_kernel(q_ref, k_ref, v_ref, u_ref, *refs, hg, tq, tk, rc, q_off, t_len, s_len):
    o_ref, ssq_ref, z_ref, later_ref, hi_ref, lo_ref, r_ref, rsum_ref, acc_ref = refs[-9:]
    nkb = s_len // tk
    reps = tk // LANES
    heads = range(hg)

    def q_block(qi):
        rows = pl.ds(pl.multiple_of(qi * tq, tq), tq)
        q0 = q_off + pl.program_id(2) * t_len + qi * tq
        n_full = jnp.minimum(q0 // tk, nkb)
        n_kv = jnp.minimum((q0 + tq - 1 + tk - 1) // tk, nkb)
        r_ref[...] = jnp.zeros(r_ref.shape, F32)
        acc_ref[...] = jnp.zeros(acc_ref.shape, F32)

        def step(j, masked):
            ks = pl.ds(pl.multiple_of(j * tk, tk), tk)

            def before(c):
                qp = q0 + c * rc + lax.broadcasted_iota(jnp.int32, (rc, 1), 0)
                kp = j * tk + lax.broadcasted_iota(jnp.int32, (1, tk), 1)
                return kp < qp

            def logits(g):
                hs = slice(g * LANES, (g + 1) * LANES)
                z_ref[g] = lax.dot_general(q_ref[rows, hs], k_ref[ks, hs], (((1,), (1,)), ((), ())),
                                           preferred_element_type=F32)

            def log_fail(g):
                for c in range(tq // rc):
                    rs = slice(c * rc, (c + 1) * rc)
                    z = z_ref[g, rs, :]
                    fail = jnp.maximum(z, 0.0) + jnp.log(1.0 + jnp.exp2(-jnp.abs(z))) * LOG2E
                    if masked:
                        fail = jnp.where(before(c), fail, 0.0)
                    hi = fail.astype(BF16)
                    hi_ref[g, rs, :] = hi
                    lo_ref[g, rs, :] = (fail - hi.astype(F32)).astype(BF16)
                    rsum_ref[g, rs, :] = jnp.broadcast_to(jnp.sum(fail, axis=1, keepdims=True), (rc, LANES))
            u = u_ref[...]

            def suffix_sums(g):
                later_ref[g] = (jnp.dot(hi_ref[g], u, preferred_element_type=F32)
                                + jnp.dot(lo_ref[g], u, preferred_element_type=F32))

            def weights(g):
                for c in range(tq // rc):
                    rs = slice(c * rc, (c + 1) * rc)
                    r_prev = r_ref[g, rs, :]
                    w = jnp.exp2(z_ref[g, rs, :] - later_ref[g, rs, :] - _lane_tile(r_prev, reps))
                    if masked:
                        w = jnp.where(before(c), w, 0.0)
                    hi_ref[g, rs, :] = w.astype(BF16)
                    r_ref[g, rs, :] = r_prev + rsum_ref[g, rs, :]

            def weigh(g):
                acc_ref[g] += jnp.dot(hi_ref[g], v_ref[ks, g * LANES:(g + 1) * LANES],
                                      preferred_element_type=F32)

            for stage in (logits, log_fail, suffix_sums, weights, weigh):
                for g in heads:
                    stage(g)

        _side_effect_loop(0, n_kv - n_full, lambda t: step(n_kv - 1 - t, True))

        def more(c):
            return jnp.logical_and(c[0] >= 0, c[1] < -SB_LOG2_ZERO)

        def visit(c):
            step(c[0], False)
            return c[0] - 1, jnp.min(r_ref[...])

        lax.while_loop(more, visit, (n_full - 1, jnp.min(r_ref[...])))
        _emit_heads(o_ref, ssq_ref, rows, [acc_ref[g] for g in heads])

    _side_effect_loop(0, t_len // tq, q_block)


def _attention(kernel, q, kv_args, extra, *, batch, heads, hg, t_len, s_len, q_off, tq, tk, rc, scratch, name,
               parts, prev=None, half=0, span=1024):
    tq = min(tq, t_len)
    rc = min(rc, tq)
    span = min(span, t_len)
    nspan = t_len // span
    ngroups = heads // hg
    assert t_len % span == 0 and span % tq == 0 and tq % rc == 0 and s_len % tk == 0 and heads % hg == 0
    assert parts % ngroups == 0
    in_specs = [pl.BlockSpec((span, hg * (q.shape[-1] // heads)), lambda b, h, t: (b * nspan + t, h))]
    args = [q]
    for a, per_head in kv_args:
        if per_head:
            in_specs.append(pl.BlockSpec((s_len, hg * LANES), lambda b, h, t: (b, h)))
        else:
            in_specs.append(pl.BlockSpec((s_len, a.shape[-1]), lambda b, h, t: (b, 0)))
        args.append(a)
    for a in extra:
        in_specs.append(pl.BlockSpec(a.shape, lambda b, h, t, nd=a.ndim: (0,) * nd))
        args.append(a)
    aliases = {}
    if prev is not None:
        for out_idx, a in enumerate(prev):
            aliases[len(args)] = out_idx
            in_specs.append(pl.BlockSpec(memory_space=pl.ANY))
            args.append(a)
    m = batch * t_len
    return pl.pallas_call(
        functools.partial(kernel, hg=hg, tq=tq, tk=tk, rc=rc, q_off=q_off, t_len=span, s_len=s_len),
        grid=(batch, ngroups, nspan),
        in_specs=in_specs,
        out_specs=[pl.BlockSpec((span, hg * LANES), lambda b, h, t: (b * nspan + t, half * ngroups + h)),
                   pl.BlockSpec((None, parts // ngroups, span, LANES),
                                lambda b, h, t: (half, h, b * nspan + t, 0))],
        out_shape=[jax.ShapeDtypeStruct((m, 2 * heads * LANES), BF16),
                   jax.ShapeDtypeStruct((2, parts, m, LANES), F32)],
        scratch_shapes=scratch(hg, tq, tk),
        input_output_aliases=aliases,
        compiler_params=_params("parallel", "parallel", "parallel"),
        name=name,
    )(*args)


def mla_attention(q, k, kr, v, **kw):
    scratch = lambda hg, tq, tk: ([pltpu.VMEM((hg, tq, tk), F32), pltpu.VMEM((hg, tq, tk), BF16)]
                                  + [pltpu.VMEM((hg, tq, LANES), F32)] * 3)
    return _attention(_mla_kernel, q, [(k, True), (kr, False), (v, True)], [], scratch=scratch,
                      name="mla_attention", **kw)


def sb_attention(q, k, v, **kw):
    tk = kw["tk"]
    u = (lax.broadcasted_iota(jnp.int32, (tk, tk), 0)
         >= lax.broadcasted_iota(jnp.int32, (tk, tk), 1)).astype(BF16)
    scratch = lambda hg, tq, tk: ([pltpu.VMEM((hg, tq, tk), F32)] * 2 + [pltpu.VMEM((hg, tq, tk), BF16)] * 2
                                  + [pltpu.VMEM((hg, tq, LANES), F32)] * 3)
    return _attention(_sb_kernel, q, [(k, True), (v, True)], [u], scratch=scratch,
                      name="sb_attention", **kw)


def _rope_table(pos, rope):
    half = rope // 2
    inv_freq = jnp.power(ROPE_THETA, -jnp.arange(half, dtype=F32) / half)
    ang = pos.astype(F32)[:, None] * inv_freq[None, :]
    cos, sin = jnp.cos(ang), jnp.sin(ang)
    return jnp.concatenate([cos, cos, -sin, sin], axis=1)


def _swap_halves(a, axis=-1):
    lo, hi = jnp.split(a, 2, axis=axis)
    return jnp.concatenate([hi, lo], axis=axis)


def _prep_layer(l, dims, w_in, w_q_b, w_kv_b, w_o, w_up, w_down, g_attn, g_q_nope, g_q_rope, g_k_rope,
                g_out_mla, g_out_sb, g_mlp):
    ql, kvl, rope, nope, vdim, heads, sbw = (dims[k] for k in
                                             ("ql", "kvl", "rope", "nope", "vdim", "heads", "sbw"))
    d = w_in.shape[1]
    wt = cast_layer(jnp.swapaxes(w_in, 1, 2), l, gain=g_attn[l], gain_axis=1, tr=1024, tc=d)
    kr = wt[ql + kvl:ql + kvl + rope]
    lat_w = ql + kvl + 2 * rope
    pad = (-lat_w) % MXU_DIM
    w_lat = jnp.concatenate([wt[:ql + kvl], kr, _swap_halves(kr, axis=0), jnp.zeros((pad, d), BF16)], axis=0)
    o = ql + kvl + rope

    wq = w_q_b[l].reshape(ql, heads, nope + rope)
    wq_r = wq[:, :, nope:]
    wq = jnp.concatenate([wq[:, :, :nope], wq_r, _swap_halves(wq_r)], axis=2).reshape(ql, heads * 2 * LANES)

    wkv = w_kv_b[l].reshape(kvl, heads, nope + vdim)
    wk = wkv[:, :, :nope].reshape(kvl, heads * nope)
    wv = wkv[:, :, nope:].reshape(kvl, heads * vdim)

    scale = LOG2E * (nope + rope) ** -0.5
    qrow = (jnp.concatenate([g_q_nope[l], g_q_rope[l], _swap_halves(g_q_rope[l])]) * scale).reshape(1, -1)
    krow = jnp.concatenate([g_k_rope[l], _swap_halves(g_k_rope[l])]).reshape(1, -1)
    bf = lambda a: a.astype(BF16)
    return dict(w_lat=w_lat, w_in_t=wt, sb_rows=[(o + n * sbw, sbw) for n in range(3)],
                wq=bf(wq), wk=bf(wk), wv=bf(wv),
                w_o=cast_layer(w_o, l, gain=jnp.concatenate([g_out_mla[l], g_out_sb[l]])),
                w_up=cast_layer(w_up, l, gain=g_mlp[l]), w_down=cast_layer(w_down, l),
                qrow=qrow, krow=krow)


def _layer(x, past, tab, lw, gains, dims, leaves, layer, depth, *, batch, t_len, q_off, tq_mla,
           tk_mla, tq_sb, tk_sb, tm_lat):
    ql, kvl, rope, heads, sbh = (dims[k] for k in ("ql", "kvl", "rope", "heads", "sbh"))
    prev = leaves if leaves is not None else (None,) * 4
    slot = lambda n: Stack(prev[n], layer, depth)
    xb, x_ssq = cast_ssq(x)
    sb_scale = LOG2E * (dims["sbw"] // sbh) ** -0.5
    in_proj = functools.partial(matmul, xb, ssq=x_ssq, w_out_major=True)
    q_rows, k_rows, v_rows = lw["sb_rows"]
    (sq,) = in_proj(lw["w_in_t"], [BF16], w_rows=q_rows, out_scale=sb_scale)
    sk, skb = in_proj(lw["w_in_t"], [F32, BF16], w_rows=k_rows, stack=slot(2))
    sv, svb = in_proj(lw["w_in_t"], [F32, BF16], w_rows=v_rows, stack=slot(3))
    (lat,) = in_proj(lw["w_lat"], [F32])
    q_mla, ckv, ckvb, krope, kropeb = latent_post(
        lat, gains["g_q_a"].reshape(1, -1), lw["wq"], lw["qrow"], gains["g_kv_a"].reshape(1, -1),
        lw["krow"], tab, ql=ql, kvl=kvl, heads=heads, rope=rope, tm=tm_lat, stack_ckv=slot(0), stack_kr=slot(1))

    if past is None:
        s_len = t_len
        c_all, kr_all, sbk_all, sbv_all = ckvb, kropeb, skb, svb
    else:
        p_ckv, p_kr, p_sbk, p_sbv = past
        past_len = p_ckv.shape[1]
        s_len = -(-(past_len + t_len) // MXU_DIM) * MXU_DIM
        fill = s_len - past_len - t_len

        def rows(cached, new):
            width = new.shape[-1]
            return jnp.concatenate([cached.astype(BF16), new.reshape(batch, t_len, width),
                                    jnp.zeros((batch, fill, width), BF16)], axis=1).reshape(batch * s_len, width)

        c_all = rows(p_ckv, ckvb)
        kr_all = rows(jnp.pad(p_kr, ((0, 0), (0, 0), (0, LANES - rope))), kropeb)
        sbk_all = key_rows(p_sbk, layer, skb, t_len=t_len, s_len=s_len)
        sbv_all = key_rows(p_sbv, layer, svb, t_len=t_len, s_len=s_len)

    k_mla, v_mla = kv_expand(c_all, lw["wk"], lw["wv"], gains["g_k_nope"].reshape(1, -1), heads=heads)
    assert heads == sbh
    hg_mla, hg_sb = min(4, heads), min(4, sbh)
    common = dict(batch=batch, t_len=t_len, s_len=s_len, q_off=q_off, parts=heads // min(hg_mla, hg_sb))
    mixed = mla_attention(q_mla, k_mla, kr_all, v_mla, heads=heads, hg=hg_mla, tq=tq_mla, tk=tk_mla,
                          rc=32 * 8 * LANES // tk_mla, **common)
    merged, mix_ssq = sb_attention(sq, sbk_all, sbv_all, heads=sbh, hg=hg_sb, tq=tq_sb, tk=tk_sb, rc=128,
                                   prev=mixed, half=1, **common)

    h, hb, h_ssq = matmul(merged, lw["w_o"], [F32, BF16], res=x, ssq=mix_ssq, emit_ssq=True, tn=512)
    (u,) = matmul(hb, lw["w_up"], [BF16], act="relu2", ssq=h_ssq[None])
    (y,) = matmul(u, lw["w_down"], [F32], res=h, tk=4096)
    return y, (ckv, krope, sk, sv)


def kernel(x_prompt, x_sample, cache_mla_ckv, cache_mla_krope, cache_sb_k, cache_sb_v,
           g_attn, w_in, g_q_a, w_q_b, g_kv_a, w_kv_b, g_q_nope, g_q_rope, g_k_nope, g_k_rope,
           g_out_mla, g_out_sb, w_o, g_mlp, w_up, w_down):
    depth = w_in.shape[0]
    bp, tp, d = x_prompt.shape
    bs, ts, _ = x_sample.shape
    past_len = cache_mla_ckv.shape[2]
    sbh, sbd = cache_sb_k.shape[-2:]
    nope, rope = g_q_nope.shape[-1], g_q_rope.shape[-1]
    ql, kvl = g_q_a.shape[-1], g_kv_a.shape[-1]
    heads = w_q_b.shape[-1] // (nope + rope)
    vdim = w_kv_b.shape[-1] // heads - nope
    assert nope == LANES and vdim == LANES and sbd == LANES and 2 * rope == LANES
    dims = dict(ql=ql, kvl=kvl, rope=rope, nope=nope, vdim=vdim, heads=heads, sbh=sbh, sbw=sbh * sbd)

    tab_p = _rope_table(jnp.arange(tp, dtype=jnp.int32), rope)
    tab_s = _rope_table(past_len + jnp.arange(ts, dtype=jnp.int32), rope)

    hp = x_prompt.reshape(bp * tp, d)
    hs = x_sample.reshape(bs * ts, d)
    rows_p = rows_s = None
    for l in range(depth):
        lw = _prep_layer(l, dims, w_in, w_q_b, w_kv_b, w_o, w_up, w_down, g_attn, g_q_nope, g_q_rope, g_k_rope,
                         g_out_mla, g_out_sb, g_mlp)
        gains = dict(g_q_a=g_q_a[l], g_kv_a=g_kv_a[l], g_k_nope=g_k_nope[l])
        hp, rows_p = _layer(hp, None, tab_p, lw, gains, dims, rows_p, l, depth, batch=bp, t_len=tp, q_off=0,
                            tq_mla=512, tk_mla=512, tq_sb=256, tk_sb=256, tm_lat=256)
        past = (cache_mla_ckv[l], cache_mla_krope[l], cache_sb_k, cache_sb_v)
        hs, rows_s = _layer(hs, past, tab_s, lw, gains, dims, rows_s, l, depth, batch=bs, t_len=ts,
                            q_off=past_len, tq_mla=ts, tk_mla=MXU_DIM, tq_sb=ts, tk_sb=MXU_DIM, tm_lat=ts)

    def leaves(rows, b, t):
        shapes = ((kvl,), (rope,), (sbh, sbd), (sbh, sbd))
        return tuple(r.reshape(depth, b, t, *s) for r, s in zip(rows, shapes))

    return (hp.reshape(bp, tp, d), hs.reshape(bs, ts, d)) + leaves(rows_p, bp, tp) + leaves(rows_s, bs, ts)
```

```python
import functools

import jax
import jax.numpy as jnp
from jax import lax
from jax.experimental import pallas as pl
from jax.experimental.pallas import tpu as pltpu

EPS = 1e-6
NEG_INF = -1e30
CHUNK = 64
ROPE_THETA = 10000.0
LANES = 128
MXU_DIM = 256
VMEM_LIMIT = 60 * 1024 * 1024
LOG2E = 1.4426950408889634
SB_LOG2_ZERO = -160.0

F32 = jnp.float32
BF16 = jnp.bfloat16


def _pick(n, cap, mult=LANES):
    if n <= cap:
        return n
    best = None
    for d in range(mult, cap + 1, mult):
        if n % d == 0:
            best = d
    assert best is not None, (n, cap, mult)
    return best


def _params(*sem):
    return pltpu.CompilerParams(dimension_semantics=sem, vmem_limit_bytes=VMEM_LIMIT)


def _rms(x, g):
    return x * lax.rsqrt(jnp.mean(x * x, axis=-1, keepdims=True) + EPS) * g


def _lane_tile(x, reps):
    return x if reps == 1 else jnp.concatenate([x] * reps, axis=1)


def _row_ssq(x):
    return jnp.broadcast_to(jnp.sum(x * x, axis=1, keepdims=True), (x.shape[0], LANES))


def _side_effect_loop(lo, hi, fn):
    lax.fori_loop(lo, hi, lambda j, c: (fn(j), c)[1], 0)


class Stack:
    def __init__(self, prev, layer, depth):
        self.prev, self.layer, self.depth = prev, layer, depth


def _cast_ssq_kernel(x_ref, o_ref, ssq_ref):
    x = x_ref[...]
    o_ref[...] = x.astype(o_ref.dtype)
    ssq_ref[...] = _row_ssq(x)


def cast_ssq(x, tm=512):
    m, d = x.shape
    tm = _pick(m, tm, 16)
    return pl.pallas_call(
        _cast_ssq_kernel,
        grid=(m // tm,),
        in_specs=[pl.BlockSpec((tm, d), lambda i: (i, 0))],
        out_specs=[pl.BlockSpec((tm, d), lambda i: (i, 0)),
                   pl.BlockSpec((None, None, tm, LANES), lambda i: (0, 0, i, 0))],
        out_shape=[jax.ShapeDtypeStruct((m, d), BF16), jax.ShapeDtypeStruct((1, 1, m, LANES), F32)],
        compiler_params=_params("parallel"),
        name="cast_ssq",
    )(x)


def _cast_kernel(x_ref, *refs):
    o_ref = refs[-1]
    x = x_ref[...]
    if len(refs) == 2:
        x = x * refs[0][...]
    o_ref[...] = x.astype(o_ref.dtype)


def cast_layer(w, layer, gain=None, gain_axis=0, tr=512, tc=4096):
    _, r, c = w.shape
    tr = _pick(r, tr, 16)
    tc = _pick(c, tc)
    in_specs = [pl.BlockSpec((None, tr, tc), lambda i, j: (layer, i, j))]
    args = [w]
    if gain is not None and gain_axis == 0:
        in_specs.append(pl.BlockSpec((tr, 1), lambda i, j: (i, 0)))
        args.append(gain.reshape(r, 1))
    elif gain is not None:
        in_specs.append(pl.BlockSpec((1, tc), lambda i, j: (0, j)))
        args.append(gain.reshape(1, c))
    return pl.pallas_call(
        _cast_kernel,
        grid=(r // tr, c // tc),
        in_specs=in_specs,
        out_specs=pl.BlockSpec((tr, tc), lambda i, j: (i, j)),
        out_shape=jax.ShapeDtypeStruct((r, c), BF16),
        compiler_params=_params("parallel", "parallel"),
        name="cast_layer",
    )(*args)


def _mm_kernel(*refs, nk, act, groups, parts, out_scale, has_res, n_aliased, n_out, emit_ssq, w_out_major):
    x_ref, w_ref = refs[0], refs[1]
    p = 2
    ssq_in_ref = res_ref = ssq_out_ref = None
    if groups:
        ssq_in_ref = refs[p]
        p += 1
    if has_res:
        res_ref = refs[p]
        p += 1
    p += n_aliased
    out_refs = refs[p:p + n_out]
    p += n_out
    if emit_ssq:
        ssq_out_ref = refs[p]
        p += 1

    def finish(r):
        if act == "relu2":
            r = jnp.maximum(r, 0.0)
            r = r * r
        if has_res:
            r = r + res_ref[...]
        for o_ref in out_refs:
            o_ref[...] = r.astype(o_ref.dtype).reshape(o_ref.shape)
        if emit_ssq:
            ssq_out_ref[...] = _row_ssq(r)

    if groups:
        kg = x_ref.shape[1] // groups
        part = None
        for g in range(groups):
            ssq = ssq_in_ref[g, 0]
            for n in range(1, parts):
                ssq = ssq + ssq_in_ref[g, n]
            rinv = lax.rsqrt(ssq * (1.0 / kg) + EPS) * out_scale
            ks = slice(g * kg, (g + 1) * kg)
            if w_out_major:
                term = lax.dot_general(x_ref[:, ks], w_ref[:, ks], (((1,), (1,)), ((), ())),
                                       preferred_element_type=F32)
            else:
                term = jnp.dot(x_ref[:, ks], w_ref[ks, :], preferred_element_type=F32)
            term = term * _lane_tile(rinv, term.shape[1] // LANES)
            part = term if part is None else part + term
    else:
        assert not w_out_major
        part = jnp.dot(x_ref[...], w_ref[...], preferred_element_type=F32)
    if nk == 1:
        finish(part)
        return
    acc_ref = out_refs[0]
    k = pl.program_id(2)

    @pl.when(k == 0)
    def _():
        acc_ref[...] = part

    @pl.when(jnp.logical_and(k > 0, k < nk - 1))
    def _():
        acc_ref[...] += part

    @pl.when(k == nk - 1)
    def _():
        finish(acc_ref[...] + part)


def matmul(x, w, outs, act=None, res=None, ssq=None, out_scale=1.0, emit_ssq=False, stack=None,
           w_out_major=False, w_rows=None, tm=1024, tn=1024, tk=None):
    m, kdim = x.shape
    row0, n = w_rows if w_rows is not None else (0, w.shape[0] if w_out_major else w.shape[1])
    assert w_rows is None or w_out_major
    tm = _pick(m, tm, 8)
    tn = _pick(n, tn)
    tk = kdim if tk is None else _pick(kdim, tk)
    nk = kdim // tk
    grid = (m // tm, n // tn, nk)
    if w_rows is not None:
        assert row0 % 16 == 0 and tn % 16 == 0
        w_spec = pl.BlockSpec((pl.Element(tn), pl.Element(tk)),
                              lambda i, j, k: (pl.multiple_of(row0 + j * tn, 16), pl.multiple_of(k * tk, LANES)))
    elif w_out_major:
        w_spec = pl.BlockSpec((tn, tk), lambda i, j, k: (j, k))
    else:
        w_spec = pl.BlockSpec((tk, tn), lambda i, j, k: (k, j))
    in_specs = [pl.BlockSpec((tm, tk), lambda i, j, k: (i, k)), w_spec]
    args = [x, w]
    groups = parts = 0
    if ssq is not None:
        groups, parts = ssq.shape[:2]
        assert nk == 1 and kdim % (groups * LANES) == 0
        in_specs.append(pl.BlockSpec((groups, parts, tm, LANES), lambda i, j, k: (0, 0, i, 0)))
        args.append(ssq)
    else:
        assert out_scale == 1.0
    if res is not None:
        in_specs.append(pl.BlockSpec((tm, tn), lambda i, j, k: (i, j)))
        args.append(res)
    aliases = {}
    if stack is not None and stack.prev is not None:
        aliases[len(args)] = 0
        in_specs.append(pl.BlockSpec(memory_space=pl.ANY))
        args.append(stack.prev)
    out_specs, out_shapes = [], []
    for o, dtype in enumerate(outs):
        if o == 0 and stack is not None:
            layer = stack.layer
            out_specs.append(pl.BlockSpec((None, tm, tn // LANES, LANES), lambda i, j, k: (layer, i, j, 0)))
            out_shapes.append(jax.ShapeDtypeStruct((stack.depth, m, n // LANES, LANES), dtype))
        else:
            out_specs.append(pl.BlockSpec((tm, tn), lambda i, j, k: (i, j)))
            out_shapes.append(jax.ShapeDtypeStruct((m, n), dtype))
    if emit_ssq:
        out_specs.append(pl.BlockSpec((None, tm, LANES), lambda i, j, k: (j, i, 0)))
        out_shapes.append(jax.ShapeDtypeStruct((n // tn, m, LANES), F32))
    assert nk == 1 or (act is None and stack is None and outs[0] == F32)
    return pl.pallas_call(
        functools.partial(_mm_kernel, nk=nk, act=act, groups=groups, parts=parts, out_scale=out_scale,
                          w_out_major=w_out_major,
                          has_res=res is not None, n_aliased=len(aliases), n_out=len(outs), emit_ssq=emit_ssq),
        grid=grid,
        in_specs=in_specs,
        out_specs=out_specs,
        out_shape=out_shapes,
        input_output_aliases=aliases,
        compiler_params=_params("parallel", "parallel", "arbitrary"),
        name="matmul",
    )(*args)


def _latent_kernel(lat_ref, gqa_ref, wq_ref, qrow_ref, gkva_ref, krow_ref, tab_ref, *refs,
                   ql, kvl, heads, rope):
    q_ref, ckv_ref, ckvb_ref, kr_ref, krb_ref = refs[-5:]
    lat = lat_ref[...]
    tab = tab_ref[...]

    def rotate(x, row):
        t = _rms(x, row) * tab
        return t + pltpu.roll(t, rope, axis=1)

    qn = _rms(lat[:, :ql], gqa_ref[...]).astype(BF16)
    q = jnp.dot(qn, wq_ref[...], preferred_element_type=F32)
    qrow = qrow_ref[...]
    for h in range(heads):
        base = h * 2 * LANES
        nope = _rms(q[:, base:base + LANES], qrow[:, :LANES])
        q_ref[:, base:base + LANES] = nope.astype(BF16)
        q_ref[:, base + LANES:base + 2 * LANES] = rotate(q[:, base + LANES:base + 2 * LANES],
                                                         qrow[:, LANES:]).astype(BF16)

    ckv = _rms(lat[:, ql:ql + kvl], gkva_ref[...])
    ckv_ref[...] = ckv
    ckvb_ref[...] = ckv.astype(BF16)

    kr = rotate(lat[:, ql + kvl:ql + kvl + LANES], krow_ref[...])
    kr_ref[...] = kr[:, :rope]
    lane = lax.broadcasted_iota(jnp.int32, kr.shape, 1)
    krb_ref[...] = jnp.where(lane < rope, kr, 0.0).astype(BF16)


def latent_post(lat, gqa, wq, qrow, gkva, krow, tab, *, ql, kvl, heads, rope, tm, stack_ckv, stack_kr):
    m = lat.shape[0]
    t = tab.shape[0]
    tm = min(tm, t)
    assert t % tm == 0 and m % tm == 0
    nt = t // tm
    layer, depth = stack_ckv.layer, stack_ckv.depth
    full = lambda a: pl.BlockSpec(a.shape, lambda i: (0,) * a.ndim)
    args = [lat, gqa, wq, qrow, gkva, krow, tab]
    in_specs = [pl.BlockSpec((tm, lat.shape[1]), lambda i: (i, 0)),
                full(gqa), full(wq), full(qrow), full(gkva), full(krow),
                pl.BlockSpec((tm, LANES), lambda i: (i % nt, 0))]
    aliases = {}
    for prev, out_idx in ((stack_ckv.prev, 1), (stack_kr.prev, 3)):
        if prev is not None:
            aliases[len(args)] = out_idx
            in_specs.append(pl.BlockSpec(memory_space=pl.ANY))
            args.append(prev)
    return pl.pallas_call(
        functools.partial(_latent_kernel, ql=ql, kvl=kvl, heads=heads, rope=rope),
        grid=(m // tm,),
        in_specs=in_specs,
        out_specs=[pl.BlockSpec((tm, heads * 2 * LANES), lambda i: (i, 0)),
                   pl.BlockSpec((None, tm, kvl), lambda i: (layer, i, 0)),
                   pl.BlockSpec((tm, kvl), lambda i: (i, 0)),
                   pl.BlockSpec((None, tm, rope), lambda i: (layer, i, 0)),
                   pl.BlockSpec((tm, LANES), lambda i: (i, 0))],
        out_shape=[jax.ShapeDtypeStruct((m, heads * 2 * LANES), BF16),
                   jax.ShapeDtypeStruct((depth, m, kvl), F32),
                   jax.ShapeDtypeStruct((m, kvl), BF16),
                   jax.ShapeDtypeStruct((depth, m, rope), F32),
                   jax.ShapeDtypeStruct((m, LANES), BF16)],
        input_output_aliases=aliases,
        compiler_params=_params("parallel"),
        name="latent_post",
    )(*args)


def _kv_expand_kernel(c_ref, wk_ref, wv_ref, g_ref, k_ref, v_ref, *, heads):
    c = c_ref[...]
    k = jnp.dot(c, wk_ref[...], preferred_element_type=F32)
    v = jnp.dot(c, wv_ref[...], preferred_element_type=F32)
    g = g_ref[...]
    for h in range(heads):
        sl = slice(h * LANES, (h + 1) * LANES)
        k_ref[:, sl] = _rms(k[:, sl], g).astype(BF16)
    v_ref[...] = v.astype(BF16)


def kv_expand(c, wk, wv, g, *, heads, tm=512):
    rows = c.shape[0]
    tm = _pick(rows, tm, 16)
    full = lambda a: pl.BlockSpec(a.shape, lambda i: (0,) * a.ndim)
    wide = jax.ShapeDtypeStruct((rows, heads * LANES), BF16)
    return pl.pallas_call(
        functools.partial(_kv_expand_kernel, heads=heads),
        grid=(rows // tm,),
        in_specs=[pl.BlockSpec((tm, c.shape[1]), lambda i: (i, 0)), full(wk), full(wv), full(g)],
        out_specs=[pl.BlockSpec((tm, heads * LANES), lambda i: (i, 0))] * 2,
        out_shape=[wide, wide],
        compiler_params=_params("parallel"),
        name="kv_expand",
    )(c, wk, wv, g)


def _key_rows_kernel(c_ref, n_ref, o_ref, *, past_len, t_len):
    heads = c_ref.shape[1]
    o_ref[:past_len, :] = c_ref[...].reshape(past_len, heads * LANES).astype(o_ref.dtype)
    o_ref[past_len:past_len + t_len, :] = n_ref[...]
    fill = o_ref.shape[0] - past_len - t_len
    if fill:
        o_ref[past_len + t_len:, :] = jnp.zeros((fill, o_ref.shape[1]), o_ref.dtype)


def key_rows(cache, layer, new, *, t_len, s_len):
    _, batch, past_len, heads, lanes = cache.shape
    assert lanes == LANES and s_len >= past_len + t_len
    return pl.pallas_call(
        functools.partial(_key_rows_kernel, past_len=past_len, t_len=t_len),
        grid=(batch,),
        in_specs=[pl.BlockSpec((None, None, past_len, heads, LANES), lambda b: (layer, b, 0, 0, 0)),
                  pl.BlockSpec((t_len, heads * LANES), lambda b: (b, 0))],
        out_specs=pl.BlockSpec((s_len, heads * LANES), lambda b: (b, 0)),
        out_shape=jax.ShapeDtypeStruct((batch * s_len, heads * LANES), BF16),
        compiler_params=_params("parallel"),
        name="key_rows",
    )(cache, new)


def _emit_heads(o_ref, ssq_ref, rows, outs):
    ssq = None
    for g, o in enumerate(outs):
        o_ref[rows, g * LANES:(g + 1) * LANES] = o.astype(o_ref.dtype)
        ssq = _row_ssq(o) if ssq is None else ssq + _row_ssq(o)
    ssq_ref[0, rows, :] = ssq
    for n in range(1, ssq_ref.shape[0]):
        ssq_ref[n, rows, :] = jnp.zeros_like(ssq)


def _mla_kernel(q_ref, k_ref, kr_ref, v_ref, *refs, hg, tq, tk, rc, q_off, t_len, s_len):
    o_ref, ssq_ref, s_ref, p_ref, m_ref, l_ref, acc_ref = refs[-7:]
    nkb = s_len // tk
    heads = range(hg)

    def q_block(qi):
        rows = pl.ds(pl.multiple_of(qi * tq, tq), tq)
        q0 = q_off + pl.program_id(2) * t_len + qi * tq
        n_full = jnp.minimum(((q0 // CHUNK + 1) * CHUNK) // tk, nkb)
        n_kv = jnp.minimum((((q0 + tq - 1) // CHUNK + 1) * CHUNK + tk - 1) // tk, nkb)
        m_ref[...] = jnp.full(m_ref.shape, NEG_INF, F32)
        l_ref[...] = jnp.zeros(l_ref.shape, F32)
        acc_ref[...] = jnp.zeros(acc_ref.shape, F32)

        def step(j, masked):
            ks = pl.ds(pl.multiple_of(j * tk, tk), tk)
            kr = kr_ref[ks, :]

            def scores(g):
                k = jnp.concatenate([k_ref[ks, g * LANES:(g + 1) * LANES], kr], axis=1)
                q = q_ref[rows, g * 2 * LANES:(g + 1) * 2 * LANES]
                s_ref[g] = lax.dot_general(q, k, (((1,), (1,)), ((), ())),
                                           preferred_element_type=F32)

            def softmax(g):
                for c in range(tq // rc):
                    rs = slice(c * rc, (c + 1) * rc)
                    s = s_ref[g, rs, :]
                    if masked:
                        qc = (q0 + c * rc + lax.broadcasted_iota(jnp.int32, (rc, 1), 0)) // CHUNK
                        kc = (j * tk + lax.broadcasted_iota(jnp.int32, (1, tk), 1)) // CHUNK
                        s = jnp.where(kc <= qc, s, NEG_INF)
                    m_prev = m_ref[g, rs, :]
                    m_next = jnp.maximum(m_prev, jnp.max(s, axis=1, keepdims=True))
                    alpha = jnp.exp2(m_prev - m_next)
                    p = jnp.exp2(s - _lane_tile(m_next, tk // LANES))
                    l_ref[g, rs, :] = alpha * l_ref[g, rs, :] + jnp.sum(p, axis=1, keepdims=True)
                    m_ref[g, rs, :] = m_next
                    acc_ref[g, rs, :] = alpha * acc_ref[g, rs, :]
                    p_ref[g, rs, :] = p.astype(BF16)

            def weigh(g):
                acc_ref[g] += jnp.dot(p_ref[g], v_ref[ks, g * LANES:(g + 1) * LANES], preferred_element_type=F32)

            for stage in (scores, softmax, weigh):
                for g in heads:
                    stage(g)

        _side_effect_loop(0, n_full, lambda j: step(j, False))
        _side_effect_loop(n_full, n_kv, lambda j: step(j, True))
        _emit_heads(o_ref, ssq_ref, rows, [acc_ref[g] / l_ref[g] for g in heads])

    _side_effect_loop(0, t_len // tq, q_block)


def _sb_kernel(q_ref, k_ref, v_ref, u_ref, *refs, hg, tq, tk, rc, q_off, t_len, s_len):
    o_ref, ssq_ref, z_ref, later_ref, hi_ref, lo_ref, r_ref, rsum_ref, acc_ref = refs[-9:]
    nkb = s_len // tk
    reps = tk // LANES
    heads = range(hg)

    def q_block(qi):
        rows = pl.ds(pl.multiple_of(qi * tq, tq), tq)
        q0 = q_off + pl.program_id(2) * t_len + qi * tq
        n_full = jnp.minimum(q0 // tk, nkb)
        n_kv = jnp.minimum((q0 + tq - 1 + tk - 1) // tk, nkb)
        r_ref[...] = jnp.zeros(r_ref.shape, F32)
        acc_ref[...] = jnp.zeros(acc_ref.shape, F32)

        def step(j, masked):
            ks = pl.ds(pl.multiple_of(j * tk, tk), tk)

            def before(c):
                qp = q0 + c * rc + lax.broadcasted_iota(jnp.int32, (rc, 1), 0)
                kp = j * tk + lax.broadcasted_iota(jnp.int32, (1, tk), 1)
                return kp < qp

            def logits(g):
                hs = slice(g * LANES, (g + 1) * LANES)
                z_ref[g] = lax.dot_general(q_ref[rows, hs], k_ref[ks, hs], (((1,), (1,)), ((), ())),
                                           preferred_element_type=F32)

            def log_fail(g):
                for c in range(tq // rc):
                    rs = slice(c * rc, (c + 1) * rc)
                    z = z_ref[g, rs, :]
                    fail = jnp.maximum(z, 0.0) + jnp.log(1.0 + jnp.exp2(-jnp.abs(z))) * LOG2E
                    if masked:
                        fail = jnp.where(before(c), fail, 0.0)
                    hi = fail.astype(BF16)
                    hi_ref[g, rs, :] = hi
                    lo_ref[g, rs, :] = (fail - hi.astype(F32)).astype(BF16)
                    rsum_ref[g, rs, :] = jnp.broadcast_to(jnp.sum(fail, axis=1, keepdims=True), (rc, LANES))
            u = u_ref[...]

            def suffix_sums(g):
                later_ref[g] = (jnp.dot(hi_ref[g], u, preferred_element_type=F32)
                                + jnp.dot(lo_ref[g], u, preferred_element_type=F32))

            def weights(g):
                for c in range(tq // rc):
                    rs = slice(c * rc, (c + 1) * rc)
                    r_prev = r_ref[g, rs, :]
                    w = jnp.exp2(z_ref[g, rs, :] - later_ref[g, rs, :] - _lane_tile(r_prev, reps))
                    if masked:
                        w = jnp.where(before(c), w, 0.0)
                    hi_ref[g, rs, :] = w.astype(BF16)
                    r_ref[g, rs, :] = r_prev + rsum_ref[g, rs, :]

            def weigh(g):
                acc_ref[g] += jnp.dot(hi_ref[g], v_ref[ks, g * LANES:(g + 1) * LANES],
                                      preferred_element_type=F32)

            for stage in (logits, log_fail, suffix_sums, weights, weigh):
                for g in heads:
                    stage(g)

        _side_effect_loop(0, n_kv - n_full, lambda t: step(n_kv - 1 - t, True))

        def more(c):
            return jnp.logical_and(c[0] >= 0, c[1] < -SB_LOG2_ZERO)

        def visit(c):
            step(c[0], False)
            return c[0] - 1, jnp.min(r_ref[...])

        lax.while_loop(more, visit, (n_full - 1, jnp.min(r_ref[...])))
        _emit_heads(o_ref, ssq_ref, rows, [acc_ref[g] for g in heads])

    _side_effect_loop(0, t_len // tq, q_block)


def _attention(kernel, q, kv_args, extra, *, batch, heads, hg, t_len, s_len, q_off, tq, tk, rc, scratch, name,
               parts, prev=None, half=0, span=1024):
    tq = min(tq, t_len)
    rc = min(rc, tq)
    span = min(span, t_len)
    nspan = t_len // span
    ngroups = heads // hg
    assert t_len % span == 0 and span % tq == 0 and tq % rc == 0 and s_len % tk == 0 and heads % hg == 0
    assert parts % ngroups == 0
    in_specs = [pl.BlockSpec((span, hg * (q.shape[-1] // heads)), lambda b, h, t: (b * nspan + t, h))]
    args = [q]
    for a, per_head in kv_args:
        if per_head:
            in_specs.append(pl.BlockSpec((s_len, hg * LANES), lambda b, h, t: (b, h)))
        else:
            in_specs.append(pl.BlockSpec((s_len, a.shape[-1]), lambda b, h, t: (b, 0)))
        args.append(a)
    for a in extra:
        in_specs.append(pl.BlockSpec(a.shape, lambda b, h, t, nd=a.ndim: (0,) * nd))
        args.append(a)
    aliases = {}
    if prev is not None:
        for out_idx, a in enumerate(prev):
            aliases[len(args)] = out_idx
            in_specs.append(pl.BlockSpec(memory_space=pl.ANY))
            args.append(a)
    m = batch * t_len
    return pl.pallas_call(
        functools.partial(kernel, hg=hg, tq=tq, tk=tk, rc=rc, q_off=q_off, t_len=span, s_len=s_len),
        grid=(batch, ngroups, nspan),
        in_specs=in_specs,
        out_specs=[pl.BlockSpec((span, hg * LANES), lambda b, h, t: (b * nspan + t, half * ngroups + h)),
                   pl.BlockSpec((None, parts // ngroups, span, LANES),
                                lambda b, h, t: (half, h, b * nspan + t, 0))],
        out_shape=[jax.ShapeDtypeStruct((m, 2 * heads * LANES), BF16),
                   jax.ShapeDtypeStruct((2, parts, m, LANES), F32)],
        scratch_shapes=scratch(hg, tq, tk),
        input_output_aliases=aliases,
        compiler_params=_params("parallel", "parallel", "parallel"),
        name=name,
    )(*args)


def mla_attention(q, k, kr, v, **kw):
    scratch = lambda hg, tq, tk: ([pltpu.VMEM((hg, tq, tk), F32), pltpu.VMEM((hg, tq, tk), BF16)]
                                  + [pltpu.VMEM((hg, tq, LANES), F32)] * 3)
    return _attention(_mla_kernel, q, [(k, True), (kr, False), (v, True)], [], scratch=scratch,
                      name="mla_attention", **kw)


def sb_attention(q, k, v, **kw):
    tk = kw["tk"]
    u = (lax.broadcasted_iota(jnp.int32, (tk, tk), 0)
         >= lax.broadcasted_iota(jnp.int32, (tk, tk), 1)).astype(BF16)
    scratch = lambda hg, tq, tk: ([pltpu.VMEM((hg, tq, tk), F32)] * 2 + [pltpu.VMEM((hg, tq, tk), BF16)] * 2
                                  + [pltpu.VMEM((hg, tq, LANES), F32)] * 3)
    return _attention(_sb_kernel, q, [(k, True), (v, True)], [u], scratch=scratch,
                      name="sb_attention", **kw)


def _rope_table(pos, rope):
    half = rope // 2
    inv_freq = jnp.power(ROPE_THETA, -jnp.arange(half, dtype=F32) / half)
    ang = pos.astype(F32)[:, None] * inv_freq[None, :]
    cos, sin = jnp.cos(ang), jnp.sin(ang)
    return jnp.concatenate([cos, cos, -sin, sin], axis=1)


def _swap_halves(a, axis=-1):
    lo, hi = jnp.split(a, 2, axis=axis)
    return jnp.concatenate([hi, lo], axis=axis)


def _prep_layer(l, dims, w_in, w_q_b, w_kv_b, w_o, w_up, w_down, g_attn, g_q_nope, g_q_rope, g_k_rope,
                g_out_mla, g_out_sb, g_mlp):
    ql, kvl, rope, nope, vdim, heads, sbw = (dims[k] for k in
                                             ("ql", "kvl", "rope", "nope", "vdim", "heads", "sbw"))
    d = w_in.shape[1]
    wt = cast_layer(jnp.swapaxes(w_in, 1, 2), l, gain=g_attn[l], gain_axis=1, tr=1024, tc=d)
    kr = wt[ql + kvl:ql + kvl + rope]
    lat_w = ql + kvl + 2 * rope
    pad = (-lat_w) % MXU_DIM
    w_lat = jnp.concatenate([wt[:ql + kvl], kr, _swap_halves(kr, axis=0), jnp.zeros((pad, d), BF16)], axis=0)
    o = ql + kvl + rope

    wq = w_q_b[l].reshape(ql, heads, nope + rope)
    wq_r = wq[:, :, nope:]
    wq = jnp.concatenate([wq[:, :, :nope], wq_r, _swap_halves(wq_r)], axis=2).reshape(ql, heads * 2 * LANES)

    wkv = w_kv_b[l].reshape(kvl, heads, nope + vdim)
    wk = wkv[:, :, :nope].reshape(kvl, heads * nope)
    wv = wkv[:, :, nope:].reshape(kvl, heads * vdim)

    scale = LOG2E * (nope + rope) ** -0.5
    qrow = (jnp.concatenate([g_q_nope[l], g_q_rope[l], _swap_halves(g_q_rope[l])]) * scale).reshape(1, -1)
    krow = jnp.concatenate([g_k_rope[l], _swap_halves(g_k_rope[l])]).reshape(1, -1)
    bf = lambda a: a.astype(BF16)
    return dict(w_lat=w_lat, w_in_t=wt, sb_rows=[(o + n * sbw, sbw) for n in range(3)],
                wq=bf(wq), wk=bf(wk), wv=bf(wv),
                w_o=cast_layer(w_o, l, gain=jnp.concatenate([g_out_mla[l], g_out_sb[l]])),
                w_up=cast_layer(w_up, l, gain=g_mlp[l]), w_down=cast_layer(w_down, l),
                qrow=qrow, krow=krow)


def _layer(x, past, tab, lw, gains, dims, leaves, layer, depth, *, batch, t_len, q_off, tq_mla,
           tk_mla, tq_sb, tk_sb, tm_lat):
    ql, kvl, rope, heads, sbh = (dims[k] for k in ("ql", "kvl", "rope", "heads", "sbh"))
    prev = leaves if leaves is not None else (None,) * 4
    slot = lambda n: Stack(prev[n], layer, depth)
    xb, x_ssq = cast_ssq(x)
    sb_scale = LOG2E * (dims["sbw"] // sbh) ** -0.5
    in_proj = functools.partial(matmul, xb, ssq=x_ssq, w_out_major=True)
    q_rows, k_rows, v_rows = lw["sb_rows"]
    (sq,) = in_proj(lw["w_in_t"], [BF16], w_rows=q_rows, out_scale=sb_scale)
    sk, skb = in_proj(lw["w_in_t"], [F32, BF16], w_rows=k_rows, stack=slot(2))
    sv, svb = in_proj(lw["w_in_t"], [F32, BF16], w_rows=v_rows, stack=slot(3))
    (lat,) = in_proj(lw["w_lat"], [F32])
    q_mla, ckv, ckvb, krope, kropeb = latent_post(
        lat, gains["g_q_a"].reshape(1, -1), lw["wq"], lw["qrow"], gains["g_kv_a"].reshape(1, -1),
        lw["krow"], tab, ql=ql, kvl=kvl, heads=heads, rope=rope, tm=tm_lat, stack_ckv=slot(0), stack_kr=slot(1))

    if past is None:
        s_len = t_len
        c_all, kr_all, sbk_all, sbv_all = ckvb, kropeb, skb, svb
    else:
        p_ckv, p_kr, p_sbk, p_sbv = past
        past_len = p_ckv.shape[1]
        s_len = -(-(past_len + t_len) // MXU_DIM) * MXU_DIM
        fill = s_len - past_len - t_len

        def rows(cached, new):
            width = new.shape[-1]
            return jnp.concatenate([cached.astype(BF16), new.reshape(batch, t_len, width),
                                    jnp.zeros((batch, fill, width), BF16)], axis=1).reshape(batch * s_len, width)

        c_all = rows(p_ckv, ckvb)
        kr_all = rows(jnp.pad(p_kr, ((0, 0), (0, 0), (0, LANES - rope))), kropeb)
        sbk_all = key_rows(p_sbk, layer, skb, t_len=t_len, s_len=s_len)
        sbv_all = key_rows(p_sbv, layer, svb, t_len=t_len, s_len=s_len)

    k_mla, v_mla = kv_expand(c_all, lw["wk"], lw["wv"], gains["g_k_nope"].reshape(1, -1), heads=heads)
    assert heads == sbh
    hg_mla, hg_sb = min(4, heads), min(4, sbh)
    common = dict(batch=batch, t_len=t_len, s_len=s_len, q_off=q_off, parts=heads // min(hg_mla, hg_sb))
    tk_mla = s_len if tk_mla is None else tk_mla
    rc_mla = max(16, 1 << ((32 * 8 * LANES // tk_mla).bit_length() - 1))
    mixed = mla_attention(q_mla, k_mla, kr_all, v_mla, heads=heads, hg=hg_mla, tq=tq_mla, tk=tk_mla,
                          rc=rc_mla, **common)
    merged, mix_ssq = sb_attention(sq, sbk_all, sbv_all, heads=sbh, hg=hg_sb, tq=tq_sb, tk=tk_sb, rc=128,
                                   prev=mixed, half=1, **common)

    h, hb, h_ssq = matmul(merged, lw["w_o"], [F32, BF16], res=x, ssq=mix_ssq, emit_ssq=True, tn=512)
    (u,) = matmul(hb, lw["w_up"], [BF16], act="relu2", ssq=h_ssq[None])
    (y,) = matmul(u, lw["w_down"], [F32], res=h, tk=4096)
    return y, (ckv, krope, sk, sv)


def kernel(x_prompt, x_sample, cache_mla_ckv, cache_mla_krope, cache_sb_k, cache_sb_v,
           g_attn, w_in, g_q_a, w_q_b, g_kv_a, w_kv_b, g_q_nope, g_q_rope, g_k_nope, g_k_rope,
           g_out_mla, g_out_sb, w_o, g_mlp, w_up, w_down):
    depth = w_in.shape[0]
    bp, tp, d = x_prompt.shape
    bs, ts, _ = x_sample.shape
    past_len = cache_mla_ckv.shape[2]
    sbh, sbd = cache_sb_k.shape[-2:]
    nope, rope = g_q_nope.shape[-1], g_q_rope.shape[-1]
    ql, kvl = g_q_a.shape[-1], g_kv_a.shape[-1]
    heads = w_q_b.shape[-1] // (nope + rope)
    vdim = w_kv_b.shape[-1] // heads - nope
    assert nope == LANES and vdim == LANES and sbd == LANES and 2 * rope == LANES
    dims = dict(ql=ql, kvl=kvl, rope=rope, nope=nope, vdim=vdim, heads=heads, sbh=sbh, sbw=sbh * sbd)

    tab_p = _rope_table(jnp.arange(tp, dtype=jnp.int32), rope)
    tab_s = _rope_table(past_len + jnp.arange(ts, dtype=jnp.int32), rope)

    hp = x_prompt.reshape(bp * tp, d)
    hs = x_sample.reshape(bs * ts, d)
    rows_p = rows_s = None
    for l in range(depth):
        lw = _prep_layer(l, dims, w_in, w_q_b, w_kv_b, w_o, w_up, w_down, g_attn, g_q_nope, g_q_rope, g_k_rope,
                         g_out_mla, g_out_sb, g_mlp)
        gains = dict(g_q_a=g_q_a[l], g_kv_a=g_kv_a[l], g_k_nope=g_k_nope[l])
        hp, rows_p = _layer(hp, None, tab_p, lw, gains, dims, rows_p, l, depth, batch=bp, t_len=tp, q_off=0,
                            tq_mla=512, tk_mla=512, tq_sb=256, tk_sb=256, tm_lat=256)
        past = (cache_mla_ckv[l], cache_mla_krope[l], cache_sb_k, cache_sb_v)
        hs, rows_s = _layer(hs, past, tab_s, lw, gains, dims, rows_s, l, depth, batch=bs, t_len=ts,
                            q_off=past_len, tq_mla=ts, tk_mla=None, tq_sb=ts, tk_sb=MXU_DIM, tm_lat=ts)

    def leaves(rows, b, t):
        shapes = ((kvl,), (rope,), (sbh, sbd), (sbh, sbd))
        return tuple(r.reshape(depth, b, t, *s) for r, s in zip(rows, shapes))

    return (hp.reshape(bp, tp, d), hs.reshape(bs, ts, d)) + leaves(rows_p, bp, tp) + leaves(rows_s, bs, ts)
```

```python
import functools

import jax
import jax.numpy as jnp
from jax import lax
from jax.experimental import pallas as pl
from jax.experimental.pallas import tpu as pltpu

EPS = 1e-6
NEG_INF = -1e30
CHUNK = 64
ROPE_THETA = 10000.0
LANES = 128
MXU_DIM = 256
VMEM_LIMIT = 60 * 1024 * 1024
LOG2E = 1.4426950408889634
SB_LOG2_ZERO = -160.0

F32 = jnp.float32
BF16 = jnp.bfloat16


def _pick(n, cap, mult=LANES):
    if n <= cap:
        return n
    best = None
    for d in range(mult, cap + 1, mult):
        if n % d == 0:
            best = d
    assert best is not None, (n, cap, mult)
    return best


def _params(*sem):
    return pltpu.CompilerParams(dimension_semantics=sem, vmem_limit_bytes=VMEM_LIMIT)


def _rms(x, g):
    return x * lax.rsqrt(jnp.mean(x * x, axis=-1, keepdims=True) + EPS) * g


def _lane_tile(x, reps):
    return x if reps == 1 else jnp.concatenate([x] * reps, axis=1)


def _row_ssq(x):
    return jnp.broadcast_to(jnp.sum(x * x, axis=1, keepdims=True), (x.shape[0], LANES))


def _side_effect_loop(lo, hi, fn):
    lax.fori_loop(lo, hi, lambda j, c: (fn(j), c)[1], 0)


class Stack:
    def __init__(self, prev, layer, depth):
        self.prev, self.layer, self.depth = prev, layer, depth


def _cast_ssq_kernel(x_ref, o_ref, ssq_ref):
    x = x_ref[...]
    o_ref[...] = x.astype(o_ref.dtype)
    ssq_ref[...] = _row_ssq(x)


def cast_ssq(x, tm=512):
    m, d = x.shape
    tm = _pick(m, tm, 16)
    return pl.pallas_call(
        _cast_ssq_kernel,
        grid=(m // tm,),
        in_specs=[pl.BlockSpec((tm, d), lambda i: (i, 0))],
        out_specs=[pl.BlockSpec((tm, d), lambda i: (i, 0)),
                   pl.BlockSpec((None, None, tm, LANES), lambda i: (0, 0, i, 0))],
        out_shape=[jax.ShapeDtypeStruct((m, d), BF16), jax.ShapeDtypeStruct((1, 1, m, LANES), F32)],
        compiler_params=_params("parallel"),
        name="cast_ssq",
    )(x)


def _cast_kernel(x_ref, *refs):
    o_ref = refs[-1]
    x = x_ref[...]
    if len(refs) == 2:
        x = x * refs[0][...]
    o_ref[...] = x.astype(o_ref.dtype)


def cast_layer(w, layer, gain=None, gain_axis=0, tr=512, tc=4096):
    _, r, c = w.shape
    tr = _pick(r, tr, 16)
    tc = _pick(c, tc)
    in_specs = [pl.BlockSpec((None, tr, tc), lambda i, j: (layer, i, j))]
    args = [w]
    if gain is not None and gain_axis == 0:
        in_specs.append(pl.BlockSpec((tr, 1), lambda i, j: (i, 0)))
        args.append(gain.reshape(r, 1))
    elif gain is not None:
        in_specs.append(pl.BlockSpec((1, tc), lambda i, j: (0, j)))
        args.append(gain.reshape(1, c))
    return pl.pallas_call(
        _cast_kernel,
        grid=(r // tr, c // tc),
        in_specs=in_specs,
        out_specs=pl.BlockSpec((tr, tc), lambda i, j: (i, j)),
        out_shape=jax.ShapeDtypeStruct((r, c), BF16),
        compiler_params=_params("parallel", "parallel"),
        name="cast_layer",
    )(*args)


def _mm_kernel(*refs, nk, act, groups, parts, out_scale, has_res, n_aliased, n_out, emit_ssq, w_out_major):
    x_ref, w_ref = refs[0], refs[1]
    p = 2
    ssq_in_ref = res_ref = ssq_out_ref = None
    if groups:
        ssq_in_ref = refs[p]
        p += 1
    if has_res:
        res_ref = refs[p]
        p += 1
    p += n_aliased
    out_refs = refs[p:p + n_out]
    p += n_out
    if emit_ssq:
        ssq_out_ref = refs[p]
        p += 1

    def finish(r):
        if act == "relu2":
            r = jnp.maximum(r, 0.0)
            r = r * r
        if has_res:
            r = r + res_ref[...]
        for o_ref in out_refs:
            o_ref[...] = r.astype(o_ref.dtype).reshape(o_ref.shape)
        if emit_ssq:
            ssq_out_ref[...] = _row_ssq(r)

    if groups:
        kg = x_ref.shape[1] // groups
        part = None
        for g in range(groups):
            ssq = ssq_in_ref[g, 0]
            for n in range(1, parts):
                ssq = ssq + ssq_in_ref[g, n]
            rinv = lax.rsqrt(ssq * (1.0 / kg) + EPS) * out_scale
            ks = slice(g * kg, (g + 1) * kg)
            if w_out_major:
                term = lax.dot_general(x_ref[:, ks], w_ref[:, ks], (((1,), (1,)), ((), ())),
                                       preferred_element_type=F32)
            else:
                term = jnp.dot(x_ref[:, ks], w_ref[ks, :], preferred_element_type=F32)
            term = term * _lane_tile(rinv, term.shape[1] // LANES)
            part = term if part is None else part + term
    else:
        assert not w_out_major
        part = jnp.dot(x_ref[...], w_ref[...], preferred_element_type=F32)
    if nk == 1:
        finish(part)
        return
    acc_ref = out_refs[0]
    k = pl.program_id(2)

    @pl.when(k == 0)
    def _():
        acc_ref[...] = part

    @pl.when(jnp.logical_and(k > 0, k < nk - 1))
    def _():
        acc_ref[...] += part

    @pl.when(k == nk - 1)
    def _():
        finish(acc_ref[...] + part)


def matmul(x, w, outs, act=None, res=None, ssq=None, out_scale=1.0, emit_ssq=False, stack=None,
           w_out_major=False, w_rows=None, tm=1024, tn=1024, tk=None):
    m, kdim = x.shape
    row0, n = w_rows if w_rows is not None else (0, w.shape[0] if w_out_major else w.shape[1])
    assert w_rows is None or w_out_major
    tm = _pick(m, tm, 8)
    tn = _pick(n, tn)
    tk = kdim if tk is None else _pick(kdim, tk)
    nk = kdim // tk
    grid = (m // tm, n // tn, nk)
    if w_rows is not None:
        assert row0 % 16 == 0 and tn % 16 == 0
        w_spec = pl.BlockSpec((pl.Element(tn), pl.Element(tk)),
                              lambda i, j, k: (pl.multiple_of(row0 + j * tn, 16), pl.multiple_of(k * tk, LANES)))
    elif w_out_major:
        w_spec = pl.BlockSpec((tn, tk), lambda i, j, k: (j, k))
    else:
        w_spec = pl.BlockSpec((tk, tn), lambda i, j, k: (k, j))
    in_specs = [pl.BlockSpec((tm, tk), lambda i, j, k: (i, k)), w_spec]
    args = [x, w]
    groups = parts = 0
    if ssq is not None:
        groups, parts = ssq.shape[:2]
        assert nk == 1 and kdim % (groups * LANES) == 0
        in_specs.append(pl.BlockSpec((groups, parts, tm, LANES), lambda i, j, k: (0, 0, i, 0)))
        args.append(ssq)
    else:
        assert out_scale == 1.0
    if res is not None:
        in_specs.append(pl.BlockSpec((tm, tn), lambda i, j, k: (i, j)))
        args.append(res)
    aliases = {}
    if stack is not None and stack.prev is not None:
        aliases[len(args)] = 0
        in_specs.append(pl.BlockSpec(memory_space=pl.ANY))
        args.append(stack.prev)
    out_specs, out_shapes = [], []
    for o, dtype in enumerate(outs):
        if o == 0 and stack is not None:
            layer = stack.layer
            out_specs.append(pl.BlockSpec((None, tm, tn // LANES, LANES), lambda i, j, k: (layer, i, j, 0)))
            out_shapes.append(jax.ShapeDtypeStruct((stack.depth, m, n // LANES, LANES), dtype))
        else:
            out_specs.append(pl.BlockSpec((tm, tn), lambda i, j, k: (i, j)))
            out_shapes.append(jax.ShapeDtypeStruct((m, n), dtype))
    if emit_ssq:
        out_specs.append(pl.BlockSpec((None, tm, LANES), lambda i, j, k: (j, i, 0)))
        out_shapes.append(jax.ShapeDtypeStruct((n // tn, m, LANES), F32))
    assert nk == 1 or (act is None and stack is None and outs[0] == F32)
    return pl.pallas_call(
        functools.partial(_mm_kernel, nk=nk, act=act, groups=groups, parts=parts, out_scale=out_scale,
                          w_out_major=w_out_major,
                          has_res=res is not None, n_aliased=len(aliases), n_out=len(outs), emit_ssq=emit_ssq),
        grid=grid,
        in_specs=in_specs,
        out_specs=out_specs,
        out_shape=out_shapes,
        input_output_aliases=aliases,
        compiler_params=_params("parallel", "parallel", "arbitrary"),
        name="matmul",
    )(*args)


def _latent_kernel(lat_ref, gqa_ref, wq_ref, qrow_ref, gkva_ref, krow_ref, tab_ref, *refs,
                   ql, kvl, heads, rope):
    q_ref, ckv_ref, ckvb_ref, kr_ref, krb_ref = refs[-5:]
    lat = lat_ref[...]
    tab = tab_ref[...]

    def rotate(x, row):
        t = _rms(x, row) * tab
        return t + pltpu.roll(t, rope, axis=1)

    qn = _rms(lat[:, :ql], gqa_ref[...]).astype(BF16)
    q = jnp.dot(qn, wq_ref[...], preferred_element_type=F32)
    qrow = qrow_ref[...]
    for h in range(heads):
        base = h * 2 * LANES
        nope = _rms(q[:, base:base + LANES], qrow[:, :LANES])
        q_ref[:, base:base + LANES] = nope.astype(BF16)
        q_ref[:, base + LANES:base + 2 * LANES] = rotate(q[:, base + LANES:base + 2 * LANES],
                                                         qrow[:, LANES:]).astype(BF16)

    ckv = _rms(lat[:, ql:ql + kvl], gkva_ref[...])
    ckv_ref[...] = ckv
    ckvb_ref[...] = ckv.astype(BF16)

    kr = rotate(lat[:, ql + kvl:ql + kvl + LANES], krow_ref[...])
    kr_ref[...] = kr[:, :rope]
    lane = lax.broadcasted_iota(jnp.int32, kr.shape, 1)
    krb_ref[...] = jnp.where(lane < rope, kr, 0.0).astype(BF16)


def latent_post(lat, gqa, wq, qrow, gkva, krow, tab, *, ql, kvl, heads, rope, tm, stack_ckv, stack_kr):
    m = lat.shape[0]
    t = tab.shape[0]
    tm = min(tm, t)
    assert t % tm == 0 and m % tm == 0
    nt = t // tm
    layer, depth = stack_ckv.layer, stack_ckv.depth
    full = lambda a: pl.BlockSpec(a.shape, lambda i: (0,) * a.ndim)
    args = [lat, gqa, wq, qrow, gkva, krow, tab]
    in_specs = [pl.BlockSpec((tm, lat.shape[1]), lambda i: (i, 0)),
                full(gqa), full(wq), full(qrow), full(gkva), full(krow),
                pl.BlockSpec((tm, LANES), lambda i: (i % nt, 0))]
    aliases = {}
    for prev, out_idx in ((stack_ckv.prev, 1), (stack_kr.prev, 3)):
        if prev is not None:
            aliases[len(args)] = out_idx
            in_specs.append(pl.BlockSpec(memory_space=pl.ANY))
            args.append(prev)
    return pl.pallas_call(
        functools.partial(_latent_kernel, ql=ql, kvl=kvl, heads=heads, rope=rope),
        grid=(m // tm,),
        in_specs=in_specs,
        out_specs=[pl.BlockSpec((tm, heads * 2 * LANES), lambda i: (i, 0)),
                   pl.BlockSpec((None, tm, kvl), lambda i: (layer, i, 0)),
                   pl.BlockSpec((tm, kvl), lambda i: (i, 0)),
                   pl.BlockSpec((None, tm, rope), lambda i: (layer, i, 0)),
                   pl.BlockSpec((tm, LANES), lambda i: (i, 0))],
        out_shape=[jax.ShapeDtypeStruct((m, heads * 2 * LANES), BF16),
                   jax.ShapeDtypeStruct((depth, m, kvl), F32),
                   jax.ShapeDtypeStruct((m, kvl), BF16),
                   jax.ShapeDtypeStruct((depth, m, rope), F32),
                   jax.ShapeDtypeStruct((m, LANES), BF16)],
        input_output_aliases=aliases,
        compiler_params=_params("parallel"),
        name="latent_post",
    )(*args)


def _kv_expand_kernel(c_ref, wk_ref, wv_ref, g_ref, k_ref, v_ref, *, heads):
    c = c_ref[...]
    k = jnp.dot(c, wk_ref[...], preferred_element_type=F32)
    v = jnp.dot(c, wv_ref[...], preferred_element_type=F32)
    g = g_ref[...]
    for h in range(heads):
        sl = slice(h * LANES, (h + 1) * LANES)
        k_ref[:, sl] = _rms(k[:, sl], g).astype(BF16)
    v_ref[...] = v.astype(BF16)


def kv_expand(c, wk, wv, g, *, heads, tm=512):
    rows = c.shape[0]
    tm = _pick(rows, tm, 16)
    full = lambda a: pl.BlockSpec(a.shape, lambda i: (0,) * a.ndim)
    wide = jax.ShapeDtypeStruct((rows, heads * LANES), BF16)
    return pl.pallas_call(
        functools.partial(_kv_expand_kernel, heads=heads),
        grid=(rows // tm,),
        in_specs=[pl.BlockSpec((tm, c.shape[1]), lambda i: (i, 0)), full(wk), full(wv), full(g)],
        out_specs=[pl.BlockSpec((tm, heads * LANES), lambda i: (i, 0))] * 2,
        out_shape=[wide, wide],
        compiler_params=_params("parallel"),
        name="kv_expand",
    )(c, wk, wv, g)


def _key_rows_kernel(c_ref, n_ref, o_ref, *, past_len, t_len):
    heads = c_ref.shape[1]
    o_ref[:past_len, :] = c_ref[...].reshape(past_len, heads * LANES).astype(o_ref.dtype)
    o_ref[past_len:past_len + t_len, :] = n_ref[...]
    fill = o_ref.shape[0] - past_len - t_len
    if fill:
        o_ref[past_len + t_len:, :] = jnp.zeros((fill, o_ref.shape[1]), o_ref.dtype)


def key_rows(cache, layer, new, *, t_len, s_len):
    _, batch, past_len, heads, lanes = cache.shape
    assert lanes == LANES and s_len >= past_len + t_len
    return pl.pallas_call(
        functools.partial(_key_rows_kernel, past_len=past_len, t_len=t_len),
        grid=(batch,),
        in_specs=[pl.BlockSpec((None, None, past_len, heads, LANES), lambda b: (layer, b, 0, 0, 0)),
                  pl.BlockSpec((t_len, heads * LANES), lambda b: (b, 0))],
        out_specs=pl.BlockSpec((s_len, heads * LANES), lambda b: (b, 0)),
        out_shape=jax.ShapeDtypeStruct((batch * s_len, heads * LANES), BF16),
        compiler_params=_params("parallel"),
        name="key_rows",
    )(cache, new)


def _emit_heads(o_ref, ssq_ref, rows, outs):
    ssq = None
    for g, o in enumerate(outs):
        o_ref[rows, g * LANES:(g + 1) * LANES] = o.astype(o_ref.dtype)
        ssq = _row_ssq(o) if ssq is None else ssq + _row_ssq(o)
    ssq_ref[0, rows, :] = ssq
    for n in range(1, ssq_ref.shape[0]):
        ssq_ref[n, rows, :] = jnp.zeros_like(ssq)


def _mla_kernel(q_ref, k_ref, kr_ref, v_ref, *refs, hg, tq, tk, rc, q_off, t_len, s_len):
    o_ref, ssq_ref, s_ref, p_ref, m_ref, l_ref, acc_ref = refs[-7:]
    nkb = s_len // tk
    heads = range(hg)

    def q_block(qi):
        rows = pl.ds(pl.multiple_of(qi * tq, tq), tq)
        q0 = q_off + pl.program_id(2) * t_len + qi * tq
        n_full = jnp.minimum(((q0 // CHUNK + 1) * CHUNK) // tk, nkb)
        n_kv = jnp.minimum((((q0 + tq - 1) // CHUNK + 1) * CHUNK + tk - 1) // tk, nkb)
        m_ref[...] = jnp.full(m_ref.shape, NEG_INF, F32)
        l_ref[...] = jnp.zeros(l_ref.shape, F32)
        acc_ref[...] = jnp.zeros(acc_ref.shape, F32)

        def step(j, masked):
            ks = pl.ds(pl.multiple_of(j * tk, tk), tk)
            kr = kr_ref[ks, :]

            def scores(g):
                k = jnp.concatenate([k_ref[ks, g * LANES:(g + 1) * LANES], kr], axis=1)
                q = q_ref[rows, g * 2 * LANES:(g + 1) * 2 * LANES]
                s_ref[g] = lax.dot_general(q, k, (((1,), (1,)), ((), ())),
                                           preferred_element_type=F32)

            def softmax(g):
                for c in range(tq // rc):
                    rs = slice(c * rc, (c + 1) * rc)
                    s = s_ref[g, rs, :]
                    if masked:
                        qc = (q0 + c * rc + lax.broadcasted_iota(jnp.int32, (rc, 1), 0)) // CHUNK
                        kc = (j * tk + lax.broadcasted_iota(jnp.int32, (1, tk), 1)) // CHUNK
                        s = jnp.where(kc <= qc, s, NEG_INF)
                    m_prev = m_ref[g, rs, :]
                    m_next = jnp.maximum(m_prev, jnp.max(s, axis=1, keepdims=True))
                    alpha = jnp.exp2(m_prev - m_next)
                    p = jnp.exp2(s - _lane_tile(m_next, tk // LANES))
                    l_ref[g, rs, :] = alpha * l_ref[g, rs, :] + jnp.sum(p, axis=1, keepdims=True)
                    m_ref[g, rs, :] = m_next
                    acc_ref[g, rs, :] = alpha * acc_ref[g, rs, :]
                    p_ref[g, rs, :] = p.astype(BF16)

            def weigh(g):
                acc_ref[g] += jnp.dot(p_ref[g], v_ref[ks, g * LANES:(g + 1) * LANES], preferred_element_type=F32)

            stages = (scores, softmax, weigh)
            for t in range(hg + len(stages) - 1):
                for n, stage in enumerate(stages):
                    if 0 <= t - n < hg:
                        stage(t - n)

        _side_effect_loop(0, n_full, lambda j: step(j, False))
        _side_effect_loop(n_full, n_kv, lambda j: step(j, True))
        _emit_heads(o_ref, ssq_ref, rows, [acc_ref[g] / l_ref[g] for g in heads])

    _side_effect_loop(0, t_len // tq, q_block)


def _sb_kernel(q_ref, k_ref, v_ref, u_ref, *refs, hg, tq, tk, rc, q_off, t_len, s_len):
    o_ref, ssq_ref, z_ref, later_ref, hi_ref, lo_ref, r_ref, rsum_ref, acc_ref = refs[-9:]
    nkb = s_len // tk
    reps = tk // LANES
    heads = range(hg)

    def q_block(qi):
        rows = pl.ds(pl.multiple_of(qi * tq, tq), tq)
        q0 = q_off + pl.program_id(2) * t_len + qi * tq
        n_full = jnp.minimum(q0 // tk, nkb)
        n_kv = jnp.minimum((q0 + tq - 1 + tk - 1) // tk, nkb)
        r_ref[...] = jnp.zeros(r_ref.shape, F32)
        acc_ref[...] = jnp.zeros(acc_ref.shape, F32)

        def step(j, masked):
            ks = pl.ds(pl.multiple_of(j * tk, tk), tk)

            def before(c):
                qp = q0 + c * rc + lax.broadcasted_iota(jnp.int32, (rc, 1), 0)
                kp = j * tk + lax.broadcasted_iota(jnp.int32, (1, tk), 1)
                return kp < qp

            def logits(g):
                hs = slice(g * LANES, (g + 1) * LANES)
                z_ref[g] = lax.dot_general(q_ref[rows, hs], k_ref[ks, hs], (((1,), (1,)), ((), ())),
                                           preferred_element_type=F32)

            def log_fail(g):
                for c in range(tq // rc):
                    rs = slice(c * rc, (c + 1) * rc)
                    z = z_ref[g, rs, :]
                    fail = jnp.maximum(z, 0.0) + jnp.log(1.0 + jnp.exp2(-jnp.abs(z))) * LOG2E
                    if masked:
                        fail = jnp.where(before(c), fail, 0.0)
                    hi = fail.astype(BF16)
                    hi_ref[g, rs, :] = hi
                    lo_ref[g, rs, :] = (fail - hi.astype(F32)).astype(BF16)
                    rsum_ref[g, rs, :] = jnp.broadcast_to(jnp.sum(fail, axis=1, keepdims=True), (rc, LANES))
            u = u_ref[...]

            def suffix_sums(g):
                later_ref[g] = (jnp.dot(hi_ref[g], u, preferred_element_type=F32)
                                + jnp.dot(lo_ref[g], u, preferred_element_type=F32))

            def weights(g):
                for c in range(tq // rc):
                    rs = slice(c * rc, (c + 1) * rc)
                    r_prev = r_ref[g, rs, :]
                    w = jnp.exp2(z_ref[g, rs, :] - later_ref[g, rs, :] - _lane_tile(r_prev, reps))
                    if masked:
                        w = jnp.where(before(c), w, 0.0)
                    hi_ref[g, rs, :] = w.astype(BF16)
                    r_ref[g, rs, :] = r_prev + rsum_ref[g, rs, :]

            def weigh(g):
                acc_ref[g] += jnp.dot(hi_ref[g], v_ref[ks, g * LANES:(g + 1) * LANES],
                                      preferred_element_type=F32)

            for stage in (logits, log_fail, suffix_sums, weights, weigh):
                for g in heads:
                    stage(g)

        _side_effect_loop(0, n_kv - n_full, lambda t: step(n_kv - 1 - t, True))

        def more(c):
            return jnp.logical_and(c[0] >= 0, c[1] < -SB_LOG2_ZERO)

        def visit(c):
            step(c[0], False)
            return c[0] - 1, jnp.min(r_ref[...])

        lax.while_loop(more, visit, (n_full - 1, jnp.min(r_ref[...])))
        _emit_heads(o_ref, ssq_ref, rows, [acc_ref[g] for g in heads])

    _side_effect_loop(0, t_len // tq, q_block)


def _attention(kernel, q, kv_args, extra, *, batch, heads, hg, t_len, s_len, q_off, tq, tk, rc, scratch, name,
               parts, prev=None, half=0, span=1024):
    tq = min(tq, t_len)
    rc = min(rc, tq)
    span = min(span, t_len)
    nspan = t_len // span
    ngroups = heads // hg
    assert t_len % span == 0 and span % tq == 0 and tq % rc == 0 and s_len % tk == 0 and heads % hg == 0
    assert parts % ngroups == 0
    in_specs = [pl.BlockSpec((span, hg * (q.shape[-1] // heads)), lambda b, h, t: (b * nspan + t, h))]
    args = [q]
    for a, per_head in kv_args:
        if per_head:
            in_specs.append(pl.BlockSpec((s_len, hg * LANES), lambda b, h, t: (b, h)))
        else:
            in_specs.append(pl.BlockSpec((s_len, a.shape[-1]), lambda b, h, t: (b, 0)))
        args.append(a)
    for a in extra:
        in_specs.append(pl.BlockSpec(a.shape, lambda b, h, t, nd=a.ndim: (0,) * nd))
        args.append(a)
    aliases = {}
    if prev is not None:
        for out_idx, a in enumerate(prev):
            aliases[len(args)] = out_idx
            in_specs.append(pl.BlockSpec(memory_space=pl.ANY))
            args.append(a)
    m = batch * t_len
    return pl.pallas_call(
        functools.partial(kernel, hg=hg, tq=tq, tk=tk, rc=rc, q_off=q_off, t_len=span, s_len=s_len),
        grid=(batch, ngroups, nspan),
        in_specs=in_specs,
        out_specs=[pl.BlockSpec((span, hg * LANES), lambda b, h, t: (b * nspan + t, half * ngroups + h)),
                   pl.BlockSpec((None, parts // ngroups, span, LANES),
                                lambda b, h, t: (half, h, b * nspan + t, 0))],
        out_shape=[jax.ShapeDtypeStruct((m, 2 * heads * LANES), BF16),
                   jax.ShapeDtypeStruct((2, parts, m, LANES), F32)],
        scratch_shapes=scratch(hg, tq, tk),
        input_output_aliases=aliases,
        compiler_params=_params("parallel", "parallel", "parallel"),
        name=name,
    )(*args)


def mla_attention(q, k, kr, v, **kw):
    scratch = lambda hg, tq, tk: ([pltpu.VMEM((hg, tq, tk), F32), pltpu.VMEM((hg, tq, tk), BF16)]
                                  + [pltpu.VMEM((hg, tq, LANES), F32)] * 3)
    return _attention(_mla_kernel, q, [(k, True), (kr, False), (v, True)], [], scratch=scratch,
                      name="mla_attention", **kw)


def sb_attention(q, k, v, **kw):
    tk = kw["tk"]
    u = (lax.broadcasted_iota(jnp.int32, (tk, tk), 0)
         >= lax.broadcasted_iota(jnp.int32, (tk, tk), 1)).astype(BF16)
    scratch = lambda hg, tq, tk: ([pltpu.VMEM((hg, tq, tk), F32)] * 2 + [pltpu.VMEM((hg, tq, tk), BF16)] * 2
                                  + [pltpu.VMEM((hg, tq, LANES), F32)] * 3)
    return _attention(_sb_kernel, q, [(k, True), (v, True)], [u], scratch=scratch,
                      name="sb_attention", **kw)


def _rope_table(pos, rope):
    half = rope // 2
    inv_freq = jnp.power(ROPE_THETA, -jnp.arange(half, dtype=F32) / half)
    ang = pos.astype(F32)[:, None] * inv_freq[None, :]
    cos, sin = jnp.cos(ang), jnp.sin(ang)
    return jnp.concatenate([cos, cos, -sin, sin], axis=1)


def _swap_halves(a, axis=-1):
    lo, hi = jnp.split(a, 2, axis=axis)
    return jnp.concatenate([hi, lo], axis=axis)


def _prep_layer(l, dims, w_in, w_q_b, w_kv_b, w_o, w_up, w_down, g_attn, g_q_nope, g_q_rope, g_k_rope,
                g_out_mla, g_out_sb, g_mlp):
    ql, kvl, rope, nope, vdim, heads, sbw = (dims[k] for k in
                                             ("ql", "kvl", "rope", "nope", "vdim", "heads", "sbw"))
    d = w_in.shape[1]
    wt = cast_layer(jnp.swapaxes(w_in, 1, 2), l, gain=g_attn[l], gain_axis=1, tr=1024, tc=d)
    kr = wt[ql + kvl:ql + kvl + rope]
    lat_w = ql + kvl + 2 * rope
    pad = (-lat_w) % MXU_DIM
    w_lat = jnp.concatenate([wt[:ql + kvl], kr, _swap_halves(kr, axis=0), jnp.zeros((pad, d), BF16)], axis=0)
    o = ql + kvl + rope

    wq = w_q_b[l].reshape(ql, heads, nope + rope)
    wq_r = wq[:, :, nope:]
    wq = jnp.concatenate([wq[:, :, :nope], wq_r, _swap_halves(wq_r)], axis=2).reshape(ql, heads * 2 * LANES)

    wkv = w_kv_b[l].reshape(kvl, heads, nope + vdim)
    wk = wkv[:, :, :nope].reshape(kvl, heads * nope)
    wv = wkv[:, :, nope:].reshape(kvl, heads * vdim)

    scale = LOG2E * (nope + rope) ** -0.5
    qrow = (jnp.concatenate([g_q_nope[l], g_q_rope[l], _swap_halves(g_q_rope[l])]) * scale).reshape(1, -1)
    krow = jnp.concatenate([g_k_rope[l], _swap_halves(g_k_rope[l])]).reshape(1, -1)
    bf = lambda a: a.astype(BF16)
    return dict(w_lat=w_lat, w_in_t=wt, sb_rows=[(o + n * sbw, sbw) for n in range(3)],
                wq=bf(wq), wk=bf(wk), wv=bf(wv),
                w_o=cast_layer(w_o, l, gain=jnp.concatenate([g_out_mla[l], g_out_sb[l]])),
                w_up=cast_layer(w_up, l, gain=g_mlp[l]), w_down=cast_layer(w_down, l),
                qrow=qrow, krow=krow)


def _layer(x, past, tab, lw, gains, dims, leaves, layer, depth, *, batch, t_len, q_off, tq_mla,
           tk_mla, tq_sb, tk_sb, tm_lat):
    ql, kvl, rope, heads, sbh = (dims[k] for k in ("ql", "kvl", "rope", "heads", "sbh"))
    prev = leaves if leaves is not None else (None,) * 4
    slot = lambda n: Stack(prev[n], layer, depth)
    xb, x_ssq = cast_ssq(x)
    sb_scale = LOG2E * (dims["sbw"] // sbh) ** -0.5
    in_proj = functools.partial(matmul, xb, ssq=x_ssq, w_out_major=True)
    q_rows, k_rows, v_rows = lw["sb_rows"]
    (sq,) = in_proj(lw["w_in_t"], [BF16], w_rows=q_rows, out_scale=sb_scale)
    sk, skb = in_proj(lw["w_in_t"], [F32, BF16], w_rows=k_rows, stack=slot(2))
    sv, svb = in_proj(lw["w_in_t"], [F32, BF16], w_rows=v_rows, stack=slot(3))
    (lat,) = in_proj(lw["w_lat"], [F32])
    q_mla, ckv, ckvb, krope, kropeb = latent_post(
        lat, gains["g_q_a"].reshape(1, -1), lw["wq"], lw["qrow"], gains["g_kv_a"].reshape(1, -1),
        lw["krow"], tab, ql=ql, kvl=kvl, heads=heads, rope=rope, tm=tm_lat, stack_ckv=slot(0), stack_kr=slot(1))

    if past is None:
        s_len = t_len
        c_all, kr_all, sbk_all, sbv_all = ckvb, kropeb, skb, svb
    else:
        p_ckv, p_kr, p_sbk, p_sbv = past
        past_len = p_ckv.shape[1]
        s_len = -(-(past_len + t_len) // MXU_DIM) * MXU_DIM
        fill = s_len - past_len - t_len

        def rows(cached, new):
            width = new.shape[-1]
            return jnp.concatenate([cached.astype(BF16), new.reshape(batch, t_len, width),
                                    jnp.zeros((batch, fill, width), BF16)], axis=1).reshape(batch * s_len, width)

        c_all = rows(p_ckv, ckvb)
        kr_all = rows(jnp.pad(p_kr, ((0, 0), (0, 0), (0, LANES - rope))), kropeb)
        sbk_all = key_rows(p_sbk, layer, skb, t_len=t_len, s_len=s_len)
        sbv_all = key_rows(p_sbv, layer, svb, t_len=t_len, s_len=s_len)

    k_mla, v_mla = kv_expand(c_all, lw["wk"], lw["wv"], gains["g_k_nope"].reshape(1, -1), heads=heads)
    assert heads == sbh
    hg_mla, hg_sb = min(4, heads), min(8, sbh)
    common = dict(batch=batch, t_len=t_len, s_len=s_len, q_off=q_off, parts=heads // min(hg_mla, hg_sb))
    tk_mla = s_len if tk_mla is None else tk_mla
    rc_mla = max(16, 1 << ((32 * 8 * LANES // tk_mla).bit_length() - 1))
    mixed = mla_attention(q_mla, k_mla, kr_all, v_mla, heads=heads, hg=hg_mla, tq=tq_mla, tk=tk_mla,
                          rc=rc_mla, **common)
    merged, mix_ssq = sb_attention(sq, sbk_all, sbv_all, heads=sbh, hg=hg_sb, tq=tq_sb, tk=tk_sb, rc=128,
                                   prev=mixed, half=1, **common)

    h, hb, h_ssq = matmul(merged, lw["w_o"], [F32, BF16], res=x, ssq=mix_ssq, emit_ssq=True, tn=512)
    (u,) = matmul(hb, lw["w_up"], [BF16], act="relu2", ssq=h_ssq[None])
    (y,) = matmul(u, lw["w_down"], [F32], res=h, tk=4096)
    return y, (ckv, krope, sk, sv)


def kernel(x_prompt, x_sample, cache_mla_ckv, cache_mla_krope, cache_sb_k, cache_sb_v,
           g_attn, w_in, g_q_a, w_q_b, g_kv_a, w_kv_b, g_q_nope, g_q_rope, g_k_nope, g_k_rope,
           g_out_mla, g_out_sb, w_o, g_mlp, w_up, w_down):
    depth = w_in.shape[0]
    bp, tp, d = x_prompt.shape
    bs, ts, _ = x_sample.shape
    past_len = cache_mla_ckv.shape[2]
    sbh, sbd = cache_sb_k.shape[-2:]
    nope, rope = g_q_nope.shape[-1], g_q_rope.shape[-1]
    ql, kvl = g_q_a.shape[-1], g_kv_a.shape[-1]
    heads = w_q_b.shape[-1] // (nope + rope)
    vdim = w_kv_b.shape[-1] // heads - nope
    assert nope == LANES and vdim == LANES and sbd == LANES and 2 * rope == LANES
    dims = dict(ql=ql, kvl=kvl, rope=rope, nope=nope, vdim=vdim, heads=heads, sbh=sbh, sbw=sbh * sbd)

    tab_p = _rope_table(jnp.arange(tp, dtype=jnp.int32), rope)
    tab_s = _rope_table(past_len + jnp.arange(ts, dtype=jnp.int32), rope)

    hp = x_prompt.reshape(bp * tp, d)
    hs = x_sample.reshape(bs * ts, d)
    rows_p = rows_s = None
    for l in range(depth):
        lw = _prep_layer(l, dims, w_in, w_q_b, w_kv_b, w_o, w_up, w_down, g_attn, g_q_nope, g_q_rope, g_k_rope,
                         g_out_mla, g_out_sb, g_mlp)
        gains = dict(g_q_a=g_q_a[l], g_kv_a=g_kv_a[l], g_k_nope=g_k_nope[l])
        hp, rows_p = _layer(hp, None, tab_p, lw, gains, dims, rows_p, l, depth, batch=bp, t_len=tp, q_off=0,
                            tq_mla=512, tk_mla=512, tq_sb=256, tk_sb=256, tm_lat=256)
        past = (cache_mla_ckv[l], cache_mla_krope[l], cache_sb_k, cache_sb_v)
        hs, rows_s = _layer(hs, past, tab_s, lw, gains, dims, rows_s, l, depth, batch=bs, t_len=ts,
                            q_off=past_len, tq_mla=ts, tk_mla=None, tq_sb=ts, tk_sb=MXU_DIM, tm_lat=ts)

    def leaves(rows, b, t):
        shapes = ((kvl,), (rope,), (sbh, sbd), (sbh, sbd))
        return tuple(r.reshape(depth, b, t, *s) for r, s in zip(rows, shapes))

    return (hp.reshape(bp, tp, d), hs.reshape(bs, ts, d)) + leaves(rows_p, bp, tp) + leaves(rows_s, bs, ts)
```
